```python
import jax, jax.numpy as jnp
from jax import lax
import numpy as np

D_MODEL = 1024
BATCH = 16
SEQ = 4096
DEPTH = 1

CHUNK = 64
Q_BLOCK = 128
FOX_HEADS = 8
FOX_HEAD_DIM = 64
FOX_WIDTH = FOX_HEADS * FOX_HEAD_DIM
MLSTM_HEADS = 4
MLSTM_HEAD_DIM = 128
MLSTM_WIDTH = MLSTM_HEADS * MLSTM_HEAD_DIM
MIX_WIDTH = FOX_WIDTH + MLSTM_WIDTH
CONV_WIDTH = 4
D_FF = 2816
EPS = 1e-6
IN_SIZES = (FOX_WIDTH, FOX_WIDTH, FOX_WIDTH, FOX_HEADS,
            MLSTM_WIDTH, MLSTM_WIDTH, MLSTM_WIDTH, MLSTM_WIDTH, MLSTM_HEADS, MLSTM_HEADS)
D_IN = 3 * FOX_WIDTH + FOX_HEADS + 4 * MLSTM_WIDTH + 2 * MLSTM_HEADS

kernel_name = "fox_mlstm_macaron_hybrid"


def rms_norm(x, g):
    xf = x.astype(jnp.float32)
    y = xf * lax.rsqrt(jnp.mean(xf * xf, axis=-1, keepdims=True) + EPS)
    return (y * g.astype(jnp.float32)).astype(x.dtype)


def head_rms_norm(a, g):
    H, Dh = a.shape[1], a.shape[3]
    af = a.astype(jnp.float32)
    y = af * lax.rsqrt(jnp.mean(af * af, axis=-1, keepdims=True) + EPS)
    return y * g.astype(jnp.float32).reshape(H, 1, Dh)


def swiglu_ffn(x, w_gu, w_down):
    gate, up = jnp.split(x @ w_gu, 2, axis=-1)
    return (jax.nn.silu(gate) * up) @ w_down


def causal_depthwise_conv(x, w):
    K = w.shape[0]
    S = x.shape[1]
    xp = jnp.pad(x, ((0, 0), (K - 1, 0), (0, 0)))
    y = xp[:, 0:S] * w[0]
    for tap in range(1, K):
        y = y + xp[:, tap:tap + S] * w[tap]
    return y


def split_heads(a, n_heads, head_dim):
    B, S, _ = a.shape
    return a.reshape(B, S, n_heads, head_dim).transpose(0, 2, 1, 3)


def merge_heads(a):
    B, H, S, Dh = a.shape
    return a.transpose(0, 2, 1, 3).reshape(B, S, H * Dh)


def forgetting_attention(q, k, v, log_f):
    B, H, S, Dh = q.shape
    F = jnp.cumsum(log_f, axis=-1)
    scale = Dh ** -0.5
    n_blk = S // Q_BLOCK
    qb = q.reshape(B, H, n_blk, Q_BLOCK, Dh).transpose(2, 0, 1, 3, 4)
    Fb = F.reshape(B, H, n_blk, Q_BLOCK).transpose(2, 0, 1, 3)
    key_pos = jnp.arange(S)

    def block(args):
        blk, q_blk, F_blk = args
        q_pos = blk * Q_BLOCK + jnp.arange(Q_BLOCK)
        s = jnp.einsum('bhqd,bhkd->bhqk', q_blk, k, preferred_element_type=jnp.float32) * scale
        s = s + (F_blk[..., :, None] - F[..., None, :])
        s = jnp.where(key_pos[None, :] <= q_pos[:, None], s, -jnp.inf)
        p = jax.nn.softmax(s, axis=-1)
        return jnp.einsum('bhqk,bhkd->bhqd', p.astype(v.dtype), v)

    out = lax.map(block, (jnp.arange(n_blk), qb, Fb))
    return out.transpose(1, 2, 0, 3, 4).reshape(B, H, S, Dh)


def mlstm_chunkwise(q, k, v, i_pre, log_f):
    B, H, S, D = q.shape
    L = CHUNK
    nc = S // L
    f32 = jnp.float32
    q = q.astype(f32) * (D ** -0.5)
    k = k.astype(f32)
    v = v.astype(f32)

    def to_chunks(a):
        a = a.reshape(B, H, nc, L, *a.shape[3:])
        return jnp.moveaxis(a, 2, 0)

    causal = jnp.tril(jnp.ones((L, L), dtype=bool))

    def step(carry, inp):
        C, n, m = carry
        qt, kt, vt, it, ft = inp
        b = jnp.cumsum(ft, axis=-1)
        logw = b[..., :, None] - b[..., None, :] + it[..., None, :]
        logw = jnp.where(causal, logw, -jnp.inf)
        inter = b + m[..., None]
        m_t = jnp.maximum(inter, jnp.max(logw, axis=-1))
        w_intra = jnp.exp(logw - m_t[..., None])
        w_inter = jnp.exp(inter - m_t)
        qk = jnp.einsum('bhld,bhsd->bhls', qt, kt) * w_intra
        num = (w_inter[..., None] * jnp.einsum('bhld,bhde->bhle', qt, C)
               + jnp.einsum('bhls,bhse->bhle', qk, vt))
        nq = w_inter * jnp.einsum('bhld,bhd->bhl', qt, n) + jnp.sum(qk, axis=-1)
        denom = jnp.maximum(jnp.abs(nq), jnp.exp(-m_t))
        h = num / denom[..., None]
        b_last = b[..., -1]
        g = b_last[..., None] - b + it
        m_new = jnp.maximum(b_last + m, jnp.max(g, axis=-1))
        decay = jnp.exp(b_last + m - m_new)
        wk = jnp.exp(g - m_new[..., None])[..., None] * kt
        C_new = decay[..., None, None] * C + jnp.einsum('bhsd,bhse->bhde', wk, vt)
        n_new = decay[..., None] * n + jnp.sum(wk, axis=-2)
        return (C_new, n_new, m_new), h

    init = (jnp.zeros((B, H, D, D), f32), jnp.zeros((B, H, D), f32), jnp.zeros((B, H), f32))
    _, h = lax.scan(step, init, (to_chunks(q), to_chunks(k), to_chunks(v),
                                 to_chunks(i_pre), to_chunks(log_f)))
    return jnp.moveaxis(h, 0, 2).reshape(B, H, S, D)


def hybrid_mixer(h, w_in, fox_f_bias, mlstm_i_bias, mlstm_f_bias, conv_w,
                 fox_out_norm, mlstm_out_norm, w_out):
    f32 = jnp.float32
    proj = h @ w_in
    split_at = [int(c) for c in np.cumsum(IN_SIZES)[:-1]]
    fq, fk, fv, ff, mq, mk, mv, mo, mi, mf = jnp.split(proj, split_at, axis=-1)
    fox_log_f = jax.nn.log_sigmoid((ff + fox_f_bias).astype(f32)).transpose(0, 2, 1)
    fox_o = forgetting_attention(split_heads(fq, FOX_HEADS, FOX_HEAD_DIM),
                                 split_heads(fk, FOX_HEADS, FOX_HEAD_DIM),
                                 split_heads(fv, FOX_HEADS, FOX_HEAD_DIM), fox_log_f)
    fox_y = merge_heads(head_rms_norm(fox_o, fox_out_norm))
    qk_conv = jax.nn.silu(causal_depthwise_conv(jnp.concatenate([mq, mk], axis=-1), conv_w))
    mq_c, mk_c = jnp.split(qk_conv, 2, axis=-1)
    m_i = (mi + mlstm_i_bias).astype(f32).transpose(0, 2, 1)
    m_logf = jax.nn.log_sigmoid((mf + mlstm_f_bias).astype(f32)).transpose(0, 2, 1)
    ml_h = mlstm_chunkwise(split_heads(mq_c, MLSTM_HEADS, MLSTM_HEAD_DIM),
                           split_heads(mk_c, MLSTM_HEADS, MLSTM_HEAD_DIM),
                           split_heads(mv, MLSTM_HEADS, MLSTM_HEAD_DIM), m_i, m_logf)
    ml_y = jax.nn.sigmoid(mo.astype(f32)) * merge_heads(head_rms_norm(ml_h, mlstm_out_norm))
    y = jnp.concatenate([fox_y, ml_y], axis=-1).astype(h.dtype)
    return y @ w_out


def setup_inputs(seed: int = 0) -> dict:
    key = jax.random.key(seed)
    ks = jax.random.split(key, 20)
    f32 = jnp.float32

    def gain(k, n):
        return 1.0 + 0.02 * jax.random.normal(k, (DEPTH, n), f32)

    x = jax.random.normal(ks[0], (BATCH, SEQ, D_MODEL), f32)
    ffn1_norm = gain(ks[1], D_MODEL)
    ffn1_w_gu = jax.random.normal(ks[2], (DEPTH, D_MODEL, 2 * D_FF), f32) * D_MODEL ** -0.5
    ffn1_w_down = jax.random.normal(ks[3], (DEPTH, D_FF, D_MODEL), f32) * D_FF ** -0.5
    mix_norm = gain(ks[4], D_MODEL)
    w_in = jax.random.normal(ks[5], (DEPTH, D_MODEL, D_IN), f32) * D_MODEL ** -0.5
    fox_f_bias = (jnp.linspace(2.0, 5.0, FOX_HEADS, dtype=f32)[None, :]
                  + 0.1 * jax.random.normal(ks[6], (DEPTH, FOX_HEADS), f32))
    mlstm_i_bias = 0.1 * jax.random.normal(ks[7], (DEPTH, MLSTM_HEADS), f32)
    mlstm_f_bias = (jnp.linspace(3.0, 6.0, MLSTM_HEADS, dtype=f32)[None, :]
                    + 0.1 * jax.random.normal(ks[8], (DEPTH, MLSTM_HEADS), f32))
    conv_w = jax.random.normal(ks[9], (DEPTH, CONV_WIDTH, 2 * MLSTM_WIDTH), f32) * CONV_WIDTH ** -0.5
    fox_out_norm = gain(ks[10], FOX_WIDTH)
    mlstm_out_norm = gain(ks[11], MLSTM_WIDTH)
    w_out = jax.random.normal(ks[12], (DEPTH, MIX_WIDTH, D_MODEL), f32) * MIX_WIDTH ** -0.5
    ffn2_norm = gain(ks[13], D_MODEL)
    ffn2_w_gu = jax.random.normal(ks[14], (DEPTH, D_MODEL, 2 * D_FF), f32) * D_MODEL ** -0.5
    ffn2_w_down = jax.random.normal(ks[15], (DEPTH, D_FF, D_MODEL), f32) * D_FF ** -0.5
    final_norm = 1.0 + 0.02 * jax.random.normal(ks[16], (D_MODEL,), f32)
    return {"x": x, "ffn1_norm": ffn1_norm, "ffn1_w_gu": ffn1_w_gu, "ffn1_w_down": ffn1_w_down,
            "mix_norm": mix_norm, "w_in": w_in, "fox_f_bias": fox_f_bias,
            "mlstm_i_bias": mlstm_i_bias, "mlstm_f_bias": mlstm_f_bias, "conv_w": conv_w,
            "fox_out_norm": fox_out_norm, "mlstm_out_norm": mlstm_out_norm, "w_out": w_out,
            "ffn2_norm": ffn2_norm, "ffn2_w_gu": ffn2_w_gu, "ffn2_w_down": ffn2_w_down,
            "final_norm": final_norm}


def reference(x, ffn1_norm, ffn1_w_gu, ffn1_w_down, mix_norm, w_in, fox_f_bias,
              mlstm_i_bias, mlstm_f_bias, conv_w, fox_out_norm, mlstm_out_norm, w_out,
              ffn2_norm, ffn2_w_gu, ffn2_w_down, final_norm):
    for l in range(DEPTH):
        x = x + 0.5 * swiglu_ffn(rms_norm(x, ffn1_norm[l]), ffn1_w_gu[l], ffn1_w_down[l])
        x = x + hybrid_mixer(rms_norm(x, mix_norm[l]), w_in[l], fox_f_bias[l], mlstm_i_bias[l],
                             mlstm_f_bias[l], conv_w[l], fox_out_norm[l], mlstm_out_norm[l], w_out[l])
        x = x + 0.5 * swiglu_ffn(rms_norm(x, ffn2_norm[l]), ffn2_w_gu[l], ffn2_w_down[l])
    return rms_norm(x, final_norm)
```

```python
import functools

import numpy as np
import jax
import jax.numpy as jnp
from jax import lax
from jax.experimental import pallas as pl
from jax.experimental.pallas import tpu as pltpu

F32 = jnp.float32
BF16 = jnp.bfloat16
EPS = 1e-6

FOX_HEADS = 8
FOX_DH = 64
FOX_W = FOX_HEADS * FOX_DH
ML_HEADS = 4
ML_DH = 128
ML_W = ML_HEADS * ML_DH
CONV_K = 4
N_GATES = FOX_HEADS + 2 * ML_HEADS
LANES = 128
HALO = 8
AUG = 3

VMEM_LIMIT = 56 * 1024 * 1024

TILES = dict(tm=512, tg=512, tq=256, L=256, ck=256)

_NT = (((1,), (1,)), ((), ()))


def _const_spec(shape):
    nd = len(shape)
    return pl.BlockSpec(shape, lambda *_: (0,) * nd, pipeline_mode=pl.Buffered(1))


def _rms(x, g):
    return x * lax.rsqrt(jnp.mean(x * x, axis=-1, keepdims=True) + EPS) * g


def _sigmoid(x):
    return 1.0 / (1.0 + jnp.exp(-x))


def _swiglu(h, wgu_ref, wd_ref, act_ref, ck):
    d_ff = wd_ref.shape[0]
    for c in range(d_ff // ck):
        g = jnp.dot(h, wgu_ref[:, c * ck:(c + 1) * ck], preferred_element_type=F32)
        u = jnp.dot(h, wgu_ref[:, d_ff + c * ck:d_ff + (c + 1) * ck], preferred_element_type=F32)
        act_ref[:, c * ck:(c + 1) * ck] = (g * _sigmoid(g) * u).astype(BF16)
    return jnp.dot(act_ref[...], wd_ref[...], preferred_element_type=F32)


def _split3(x):
    hi = x.astype(BF16)
    r = x - hi.astype(F32)
    mid = r.astype(BF16)
    lo = (r - mid.astype(F32)).astype(BF16)
    return jnp.concatenate([hi, mid, lo], axis=-1)


def _front_kernel(x_ref, g1_ref, wgu_ref, wd_ref, gm_ref, wf_ref, wm_ref, wg_ref, cw_ref,
                  x1_ref, fq_ref, fk_ref, fv_ref, mq_ref, mk_ref, mv_ref, mo_ref, gate_ref,
                  act_ref, cbuf_ref, *, ck):
    tm = x_ref.shape[1]
    x = x_ref[0]
    h = _rms(x, g1_ref[...]).astype(BF16)
    x1 = x + 0.5 * _swiglu(h, wgu_ref, wd_ref, act_ref, ck)
    x1_ref[0] = x1
    h2 = _rms(x1, gm_ref[...]).astype(BF16)

    def proj(w_ref, lo, hi):
        return jnp.dot(h2, w_ref[:, lo:hi], preferred_element_type=F32)

    fq_ref[0] = (proj(wf_ref, 0, FOX_W) * FOX_DH ** -0.5).astype(BF16)
    fk_ref[0] = proj(wf_ref, FOX_W, 2 * FOX_W).astype(BF16)
    fv_ref[0] = proj(wf_ref, 2 * FOX_W, 3 * FOX_W).astype(BF16)

    @pl.when(pl.program_id(1) == 0)
    def _():
        cbuf_ref[0:HALO, :] = jnp.zeros((HALO, 2 * ML_W), F32)

    cbuf_ref[HALO:HALO + tm, :] = proj(wm_ref, 0, 2 * ML_W)
    y = cbuf_ref[HALO:HALO + tm, :] * cw_ref[CONV_K - 1:CONV_K, :]
    for tap in range(CONV_K - 1):
        off = HALO - (CONV_K - 1) + tap
        y = y + cbuf_ref[off:off + tm, :] * cw_ref[tap:tap + 1, :]
    cbuf_ref[0:HALO, :] = cbuf_ref[tm:tm + HALO, :]
    y = y * _sigmoid(y)
    mq_ref[0] = (y[:, :ML_W] * ML_DH ** -0.5).astype(BF16)
    mk_ref[0] = y[:, ML_W:].astype(BF16)
    mv_ref[0] = proj(wm_ref, 2 * ML_W, 3 * ML_W).astype(BF16)
    mo_ref[0] = _sigmoid(proj(wm_ref, 3 * ML_W, 4 * ML_W)).astype(BF16)
    gate_ref[0] = jnp.dot(h2, wg_ref[...], preferred_element_type=F32)


def _front(x, g1, wgu, wd, gm, wf, wm, wg, cw, *, tm, ck):
    B, S, D = x.shape
    d_ff = wd.shape[0]
    tok = lambda w: pl.BlockSpec((1, tm, w), lambda b, i: (b, i, 0))
    bf = lambda w: jax.ShapeDtypeStruct((B, S, w), BF16)
    return pl.pallas_call(
        functools.partial(_front_kernel, ck=ck),
        grid=(B, S // tm),
        in_specs=[tok(D), _const_spec(g1.shape), _const_spec(wgu.shape), _const_spec(wd.shape),
                  _const_spec(gm.shape), _const_spec(wf.shape), _const_spec(wm.shape),
                  _const_spec(wg.shape), _const_spec(cw.shape)],
        out_specs=[tok(D)] + [tok(FOX_W)] * 3 + [tok(ML_W)] * 4 + [tok(LANES)],
        out_shape=[jax.ShapeDtypeStruct((B, S, D), F32)] + [bf(FOX_W)] * 3 + [bf(ML_W)] * 4
                  + [jax.ShapeDtypeStruct((B, S, LANES), F32)],
        scratch_shapes=[pltpu.VMEM((tm, d_ff), BF16), pltpu.VMEM((tm + HALO, 2 * ML_W), F32)],
        compiler_params=pltpu.CompilerParams(
            dimension_semantics=("parallel", "arbitrary"), vmem_limit_bytes=VMEM_LIMIT),
        name="front",
    )(x, g1, wgu, wd, gm, wf, wm, wg, cw)


FOX_F0 = 0
ML_I0 = FOX_HEADS
ML_F0 = FOX_HEADS + ML_HEADS


def _fox_aug_base(h):
    return FOX_DH if h % 2 == 0 else 0


def _placement_matrices():
    pq = np.zeros((AUG * LANES, FOX_HEADS * LANES), np.float32)
    pk = np.zeros_like(pq)
    cq = np.zeros((1, FOX_HEADS * LANES), np.float32)
    ck = np.zeros_like(cq)
    for h in range(FOX_HEADS):
        base = h * LANES + _fox_aug_base(h)
        for t in range(AUG):
            pq[t * LANES + FOX_F0 + h, base + t] = 1.0
            ck[0, base + t] = 1.0
            cq[0, base + AUG + t] = 1.0
            pk[t * LANES + FOX_F0 + h, base + AUG + t] = -1.0
    n = ML_HEADS * LANES
    p_m_stats = np.zeros((AUG * LANES, n), np.float32)
    p_a_stats = np.zeros_like(p_m_stats)
    p_fm_stats = np.zeros_like(p_m_stats)
    p_row = np.zeros_like(p_m_stats)
    p_col = np.zeros_like(p_m_stats)
    c_row = np.zeros((1, n), np.float32)
    c_col = np.zeros_like(c_row)
    for h in range(ML_HEADS):
        src = ML_F0 + h
        for t in range(AUG):
            p_m_stats[t * LANES + src, h * LANES + 0] = 1.0
            p_a_stats[t * LANES + src, h * LANES + 1] = 1.0
            p_fm_stats[t * LANES + src, h * LANES + 2] = 1.0
            p_row[t * LANES + src, h * LANES + AUG + t] = -1.0
            c_row[0, h * LANES + t] = 1.0
            p_col[t * LANES + src, h * LANES + t] = 1.0
            c_col[0, h * LANES + AUG + t] = 1.0
    as_bf = lambda a: jnp.asarray(a, BF16)
    as_f = lambda a: jnp.asarray(a, F32)
    return (as_bf(pq), as_bf(pk), as_f(cq), as_f(ck),
            as_bf(p_m_stats), as_bf(p_a_stats), as_bf(p_fm_stats), as_bf(p_row), as_bf(p_col),
            as_f(c_row), as_f(c_col))


def _gates_kernel(gate_ref, bias_ref, fq_ref, fk_ref,
                  pq_ref, pk_ref, cq_ref, ck_ref, pms_ref, pas_ref, pfs_ref, prow_ref, pcol_ref,
                  crow_ref, ccol_ref,
                  qa_ref, ka_ref, stat_ref, row_ref, col_ref, carry_ref):
    tg = gate_ref.shape[1]

    @pl.when(pl.program_id(1) == 0)
    def _():
        carry_ref[...] = jnp.zeros(carry_ref.shape, F32)

    z = gate_ref[0] + bias_ref[...]
    lane = lax.broadcasted_iota(jnp.int32, (tg, LANES), 1)
    row = lax.broadcasted_iota(jnp.int32, (tg, LANES), 0)
    is_forget = (lane < ML_I0) | ((lane >= ML_F0) & (lane < ML_F0 + ML_HEADS))
    log_f = jnp.minimum(z, 0.0) - jnp.log1p(jnp.exp(-jnp.abs(z)))
    log_f = jnp.where(is_forget, log_f, 0.0)

    r = lax.broadcasted_iota(jnp.int32, (tg, tg), 0)
    c = lax.broadcasted_iota(jnp.int32, (tg, tg), 1)
    tril = jnp.where(r >= c, 1.0, 0.0).astype(BF16)
    s3 = _split3(log_f)
    cum = (jnp.dot(tril, s3[:, 0:LANES], preferred_element_type=F32)
           + jnp.dot(tril, s3[:, LANES:2 * LANES], preferred_element_type=F32)
           + jnp.dot(tril, s3[:, 2 * LANES:], preferred_element_type=F32))
    F = cum + carry_ref[0:1, :]
    carry_ref[0:1, :] = F[tg - 1:tg, :]

    i_pre = pltpu.roll(z, ML_F0 - ML_I0, axis=1)
    a = i_pre - F
    m = a
    sh = 1
    while sh < tg:
        m = jnp.maximum(m, jnp.where(row >= sh, pltpu.roll(m, sh, axis=0), -jnp.inf))
        sh *= 2
    M = jnp.maximum(m, carry_ref[1:2, :])
    carry_ref[1:2, :] = M[tg - 1:tg, :]

    f3 = _split3(F)
    qa = jnp.dot(f3, pq_ref[...], preferred_element_type=F32) + cq_ref[...]
    ka = jnp.dot(f3, pk_ref[...], preferred_element_type=F32) + ck_ref[...]
    for h in range(FOX_HEADS):
        blk = slice((h // 2) * LANES, (h // 2 + 1) * LANES)
        own = (lane < FOX_DH) if h % 2 == 0 else (lane >= FOX_DH)
        qa_ref[0, h] = jnp.where(own, fq_ref[0, :, blk], qa[:, h * LANES:(h + 1) * LANES].astype(BF16))
        ka_ref[0, h] = jnp.where(own, fk_ref[0, :, blk], ka[:, h * LANES:(h + 1) * LANES].astype(BF16))

    m3 = _split3(M)
    a3 = _split3(a)
    fm3 = _split3(F + M)
    stats = (jnp.dot(m3, pms_ref[...], preferred_element_type=F32)
             + jnp.dot(a3, pas_ref[...], preferred_element_type=F32)
             + jnp.dot(fm3, pfs_ref[...], preferred_element_type=F32))
    rowop = jnp.dot(m3, prow_ref[...], preferred_element_type=F32) + crow_ref[...]
    colop = jnp.dot(a3, pcol_ref[...], preferred_element_type=F32) + ccol_ref[...]
    for h in range(ML_HEADS):
        sl = slice(h * LANES, (h + 1) * LANES)
        stat_ref[0, h] = stats[:, sl]
        row_ref[0, h] = rowop[:, sl].astype(BF16)
        col_ref[0, h] = colop[:, sl].astype(BF16)


def _gates(gate, bias, fq, fk, mats, *, tg):
    B, S, _ = gate.shape
    tok = lambda w: pl.BlockSpec((1, tg, w), lambda b, i: (b, i, 0))
    head = lambda nh: pl.BlockSpec((1, nh, tg, LANES), lambda b, i: (b, 0, i, 0))
    return pl.pallas_call(
        _gates_kernel,
        grid=(B, S // tg),
        in_specs=[tok(LANES), _const_spec(bias.shape), tok(FOX_W), tok(FOX_W)]
                 + [_const_spec(m.shape) for m in mats],
        out_specs=[head(FOX_HEADS), head(FOX_HEADS), head(ML_HEADS), head(ML_HEADS), head(ML_HEADS)],
        out_shape=[jax.ShapeDtypeStruct((B, FOX_HEADS, S, LANES), BF16),
                   jax.ShapeDtypeStruct((B, FOX_HEADS, S, LANES), BF16),
                   jax.ShapeDtypeStruct((B, ML_HEADS, S, LANES), F32),
                   jax.ShapeDtypeStruct((B, ML_HEADS, S, LANES), BF16),
                   jax.ShapeDtypeStruct((B, ML_HEADS, S, LANES), BF16)],
        scratch_shapes=[pltpu.VMEM((HALO, LANES), F32)],
        compiler_params=pltpu.CompilerParams(
            dimension_semantics=("parallel", "arbitrary"), vmem_limit_bytes=VMEM_LIMIT),
        name="gates",
    )(gate, bias, fq, fk, *mats)


def _fox_kernel(q_ref, k_ref, v_ref, gain_ref, o_ref, *, tq):
    i = pl.program_id(2)
    lane = lax.broadcasted_iota(jnp.int32, (tq, LANES), 1)
    r = lax.broadcasted_iota(jnp.int32, (tq, tq), 0)
    c = lax.broadcasted_iota(jnp.int32, (tq, tq), 1)

    def head(hh):
        q = q_ref[0, hh]

        def step(j, carry, masked):
            m, l, acc = carry
            start = pl.multiple_of(j * tq, tq)
            kb = k_ref[0, hh, pl.ds(start, tq), :]
            vb = v_ref[0, pl.ds(start, tq), :]
            s = lax.dot_general(q, kb, _NT, preferred_element_type=F32)
            if masked:
                s = jnp.where(c <= r, s, -jnp.inf)
            m_new = jnp.maximum(m, jnp.max(s, axis=-1, keepdims=True))
            p = jnp.exp(s - m_new)
            alpha = jnp.exp(m - m_new)
            l = alpha * l + jnp.sum(p, axis=-1, keepdims=True)
            acc = alpha * acc + jnp.dot(p.astype(BF16), vb, preferred_element_type=F32)
            return m_new, l, acc

        init = (jnp.full((tq, 1), -jnp.inf, F32), jnp.zeros((tq, 1), F32),
                jnp.zeros((tq, LANES), F32))
        carry = lax.fori_loop(0, i, functools.partial(step, masked=False), init)
        m, l, acc = step(i, carry, True)
        return acc / l

    o = jnp.where(lane < FOX_DH, head(0), head(1))
    sq = o * o
    ss_lo = jnp.sum(jnp.where(lane < FOX_DH, sq, 0.0), axis=-1, keepdims=True)
    ss_hi = jnp.sum(sq, axis=-1, keepdims=True) - ss_lo
    ms = jnp.where(lane < FOX_DH, ss_lo, ss_hi) * (1.0 / FOX_DH)
    o_ref[0] = (o * lax.rsqrt(ms + EPS) * gain_ref[...]).astype(o_ref.dtype)


def _fox(qa, ka, fv, gain, *, tq):
    B, _, S, _ = qa.shape
    return pl.pallas_call(
        functools.partial(_fox_kernel, tq=tq),
        grid=(B, FOX_HEADS // 2, S // tq),
        in_specs=[pl.BlockSpec((1, 2, tq, LANES), lambda b, p, i: (b, p, i, 0)),
                  pl.BlockSpec((1, 2, S, LANES), lambda b, p, i: (b, p, 0, 0)),
                  pl.BlockSpec((1, S, LANES), lambda b, p, i: (b, 0, p)),
                  pl.BlockSpec((1, LANES), lambda b, p, i: (0, p))],
        out_specs=pl.BlockSpec((1, tq, LANES), lambda b, p, i: (b, i, p)),
        out_shape=jax.ShapeDtypeStruct((B, S, FOX_W), BF16),
        compiler_params=pltpu.CompilerParams(
            dimension_semantics=("parallel", "parallel", "arbitrary"), vmem_limit_bytes=VMEM_LIMIT),
        name="fox",
    )(qa, ka, fv, gain)


def _mlstm_kernel(q_ref, k_ref, v_ref, o_ref, stat_ref, row_ref, col_ref, gain_ref, y_ref,
                  c_ref, mprev_ref):
    L = q_ref.shape[1]

    @pl.when(pl.program_id(2) == 0)
    def _():
        c_ref[...] = jnp.zeros(c_ref.shape, F32)
        mprev_ref[...] = jnp.zeros(mprev_ref.shape, F32)

    q = q_ref[0]
    k = k_ref[0]
    lane2 = lax.broadcasted_iota(jnp.int32, (L, LANES), 1)
    v_aug = jnp.concatenate([v_ref[0], jnp.where(lane2 == 0, 1.0, 0.0).astype(BF16)], axis=-1)
    stat = stat_ref[0, 0]
    m_col = stat[:, 0:1]
    a_col = stat[:, 1:2]
    fm_col = stat[:, 2:3]
    m_prev = mprev_ref[0:1, 0:1]
    m_end = m_col[L - 1:L, :]

    r = lax.broadcasted_iota(jnp.int32, (L, L), 0)
    c = lax.broadcasted_iota(jnp.int32, (L, L), 1)
    d_arg = lax.dot_general(row_ref[0, 0], col_ref[0, 0], _NT, preferred_element_type=F32)
    decay = jnp.where(c <= r, jnp.exp(d_arg), 0.0)
    qk = lax.dot_general(q, k, _NT, preferred_element_type=F32) * decay
    intra = jnp.dot(qk.astype(BF16), v_aug, preferred_element_type=F32)
    inter = jnp.dot(q, c_ref[...].astype(BF16), preferred_element_type=F32)
    tot = jnp.exp(m_prev - m_col) * inter + intra
    num = tot[:, :ML_DH]
    nq = tot[:, ML_DH:ML_DH + 1]
    hh = num / jnp.maximum(jnp.abs(nq), jnp.exp(-fm_col))
    y = hh * lax.rsqrt(jnp.mean(hh * hh, axis=-1, keepdims=True) + EPS) * gain_ref[...]
    y_ref[0] = (o_ref[0].astype(F32) * y).astype(y_ref.dtype)

    kw = (k.astype(F32) * jnp.exp(a_col - m_end)).T.astype(BF16)
    c_ref[...] = (jnp.exp(m_prev - m_end) * c_ref[...]
                  + jnp.dot(kw, v_aug, preferred_element_type=F32))
    mprev_ref[...] = jnp.broadcast_to(m_end, mprev_ref.shape)


def _mlstm(mq, mk, mv, mo, stat, rowop, colop, gain, *, L):
    B, S, _ = mq.shape
    tok = pl.BlockSpec((1, L, ML_DH), lambda b, h, c: (b, c, h))
    head = pl.BlockSpec((1, 1, L, LANES), lambda b, h, c: (b, h, c, 0))
    return pl.pallas_call(
        _mlstm_kernel,
        grid=(B, ML_HEADS, S // L),
        in_specs=[tok, tok, tok, tok, head, head, head,
                  pl.BlockSpec((1, ML_DH), lambda b, h, c: (0, h))],
        out_specs=tok,
        out_shape=jax.ShapeDtypeStruct((B, S, ML_W), BF16),
        scratch_shapes=[pltpu.VMEM((ML_DH, 2 * ML_DH), F32), pltpu.VMEM((HALO, LANES), F32)],
        compiler_params=pltpu.CompilerParams(
            dimension_semantics=("parallel", "parallel", "arbitrary"), vmem_limit_bytes=VMEM_LIMIT),
        name="mlstm",
    )(mq, mk, mv, mo, stat, rowop, colop, gain)


def _back_kernel(x1_ref, yf_ref, ym_ref, wo_ref, g2_ref, wgu_ref, wd_ref, gf_ref, out_ref,
                 act_ref, *, ck):
    mix = (jnp.dot(yf_ref[0], wo_ref[0:FOX_W, :], preferred_element_type=F32)
           + jnp.dot(ym_ref[0], wo_ref[FOX_W:, :], preferred_element_type=F32))
    x2 = x1_ref[0] + mix
    h = _rms(x2, g2_ref[...]).astype(BF16)
    x3 = x2 + 0.5 * _swiglu(h, wgu_ref, wd_ref, act_ref, ck)
    out_ref[0] = _rms(x3, gf_ref[...])


def _back(x1, yf, ym, wo, g2, wgu, wd, gf, *, tm, ck):
    B, S, D = x1.shape
    d_ff = wd.shape[0]
    tok = lambda w: pl.BlockSpec((1, tm, w), lambda b, i: (b, i, 0))
    return pl.pallas_call(
        functools.partial(_back_kernel, ck=ck),
        grid=(B, S // tm),
        in_specs=[tok(D), tok(FOX_W), tok(ML_W), _const_spec(wo.shape), _const_spec(g2.shape),
                  _const_spec(wgu.shape), _const_spec(wd.shape), _const_spec(gf.shape)],
        out_specs=tok(D),
        out_shape=jax.ShapeDtypeStruct((B, S, D), F32),
        scratch_shapes=[pltpu.VMEM((tm, d_ff), BF16)],
        compiler_params=pltpu.CompilerParams(
            dimension_semantics=("parallel", "parallel"), vmem_limit_bytes=VMEM_LIMIT),
        name="back",
    )(x1, yf, ym, wo, g2, wgu, wd, gf)


def _layer(x, ffn1_norm, ffn1_w_gu, ffn1_w_down, mix_norm, w_in, fox_f_bias, mlstm_i_bias,
           mlstm_f_bias, conv_w, fox_out_norm, mlstm_out_norm, w_out, ffn2_norm, ffn2_w_gu,
           ffn2_w_down, final_norm, *, tm, tg, tq, L, ck):
    row = lambda v: v.reshape(1, -1).astype(F32)
    bf = lambda w: w.astype(BF16)
    o_ff = 3 * FOX_W
    o_m = o_ff + FOX_HEADS
    o_mi = o_m + 4 * ML_W
    wf = bf(w_in[:, :o_ff])
    wm = bf(w_in[:, o_m:o_mi])
    wg = jnp.concatenate([w_in[:, o_ff:o_m], w_in[:, o_mi:]], axis=1)
    wg = bf(jnp.pad(wg, ((0, 0), (0, LANES - N_GATES))))
    bias = jnp.pad(jnp.concatenate([fox_f_bias, mlstm_i_bias, mlstm_f_bias]), (0, LANES - N_GATES))

    x1, fq, fk, fv, mq, mk, mv, mo, gate = _front(
        x, row(ffn1_norm), bf(ffn1_w_gu), bf(ffn1_w_down), row(mix_norm), wf, wm, wg,
        conv_w.astype(F32), tm=tm, ck=ck)
    qa, ka, stat, rowop, colop = _gates(gate, row(bias), fq, fk, _placement_matrices(), tg=tg)
    yf = _fox(qa, ka, fv, row(fox_out_norm), tq=tq)
    ym = _mlstm(mq, mk, mv, mo, stat, rowop, colop, row(mlstm_out_norm), L=L)
    return _back(x1, yf, ym, bf(w_out), row(ffn2_norm), bf(ffn2_w_gu), bf(ffn2_w_down),
                 row(final_norm), tm=tm, ck=ck)


def kernel(x, ffn1_norm, ffn1_w_gu, ffn1_w_down, mix_norm, w_in, fox_f_bias, mlstm_i_bias,
           mlstm_f_bias, conv_w, fox_out_norm, mlstm_out_norm, w_out, ffn2_norm, ffn2_w_gu,
           ffn2_w_down, final_norm):
    assert ffn1_norm.shape[0] == 1, "the back kernel fuses the final norm: depth 1 only"
    return _layer(x, ffn1_norm[0], ffn1_w_gu[0], ffn1_w_down[0], mix_norm[0], w_in[0],
                  fox_f_bias[0], mlstm_i_bias[0], mlstm_f_bias[0], conv_w[0], fox_out_norm[0],
                  mlstm_out_norm[0], w_out[0], ffn2_norm[0], ffn2_w_gu[0], ffn2_w_down[0],
                  final_norm, **TILES)
```

```python
import functools

import numpy as np
import jax
import jax.numpy as jnp
from jax import lax
from jax.experimental import pallas as pl
from jax.experimental.pallas import tpu as pltpu

F32 = jnp.float32
BF16 = jnp.bfloat16
EPS = 1e-6
LOG2E = 1.4426950408889634

FOX_HEADS = 8
FOX_DH = 64
FOX_W = FOX_HEADS * FOX_DH
ML_HEADS = 4
ML_DH = 128
ML_W = ML_HEADS * ML_DH
CONV_K = 4
N_GATES = FOX_HEADS + 2 * ML_HEADS
LANES = 128
HALO = 8
AUG = 3

VMEM_LIMIT = 56 * 1024 * 1024

TILES = dict(tm=512, tg=512, tq=512, tk=256, L=256, ck=256)

_NT = (((1,), (1,)), ((), ()))


def _const_spec(shape):
    nd = len(shape)
    return pl.BlockSpec(shape, lambda *_: (0,) * nd, pipeline_mode=pl.Buffered(1))


def _rms(x, g):
    return x * lax.rsqrt(jnp.mean(x * x, axis=-1, keepdims=True) + EPS) * g


def _sigmoid(x):
    return 1.0 / (1.0 + jnp.exp(-x))


def _swiglu(h, wgu_ref, wd_ref, act_ref, ck):
    d_ff = wd_ref.shape[0]
    for c in range(d_ff // ck):
        g = jnp.dot(h, wgu_ref[:, c * ck:(c + 1) * ck], preferred_element_type=F32)
        u = jnp.dot(h, wgu_ref[:, d_ff + c * ck:d_ff + (c + 1) * ck], preferred_element_type=F32)
        act_ref[:, c * ck:(c + 1) * ck] = (g * _sigmoid(g) * u).astype(BF16)
    return jnp.dot(act_ref[...], wd_ref[...], preferred_element_type=F32)


def _split3(x):
    hi = x.astype(BF16)
    r = x - hi.astype(F32)
    mid = r.astype(BF16)
    lo = (r - mid.astype(F32)).astype(BF16)
    return jnp.concatenate([hi, mid, lo], axis=-1)


def _front_kernel(x_ref, g1_ref, wgu_ref, wd_ref, gm_ref, wqt_ref, wk_ref, wvt_ref, wm_ref, wg_ref,
                  cw_ref,
                  x1_ref, fqt_ref, fk_ref, fvt_ref, mq_ref, mk_ref, mv_ref, mo_ref, gate_ref,
                  act_ref, cbuf_ref, *, ck):
    tm = x_ref.shape[1]
    tk = fvt_ref.shape[3]
    x = x_ref[0]
    h = _rms(x, g1_ref[...]).astype(BF16)
    x1 = x + 0.5 * _swiglu(h, wgu_ref, wd_ref, act_ref, ck)
    x1_ref[0] = x1
    h2 = _rms(x1, gm_ref[...]).astype(BF16)

    def proj(w_ref, lo, hi):
        return jnp.dot(h2, w_ref[:, lo:hi], preferred_element_type=F32)

    fqt = lax.dot_general(wqt_ref[...], h2, _NT, preferred_element_type=F32)
    fqt_ref[0] = (fqt * (FOX_DH ** -0.5 * LOG2E)).astype(BF16)
    fvt = lax.dot_general(wvt_ref[...], h2, _NT, preferred_element_type=F32).astype(BF16)
    for c in range(tm // tk):
        fvt_ref[0, c] = fvt[:, c * tk:(c + 1) * tk]
    fk_ref[0] = jnp.dot(h2, wk_ref[...], preferred_element_type=F32).astype(BF16)

    @pl.when(pl.program_id(1) == 0)
    def _():
        cbuf_ref[0:HALO, :] = jnp.zeros((HALO, 2 * ML_W), F32)

    cbuf_ref[HALO:HALO + tm, :] = proj(wm_ref, 0, 2 * ML_W)
    y = cbuf_ref[HALO:HALO + tm, :] * cw_ref[CONV_K - 1:CONV_K, :]
    for tap in range(CONV_K - 1):
        off = HALO - (CONV_K - 1) + tap
        y = y + cbuf_ref[off:off + tm, :] * cw_ref[tap:tap + 1, :]
    cbuf_ref[0:HALO, :] = cbuf_ref[tm:tm + HALO, :]
    y = y * _sigmoid(y)
    mq_ref[0] = (y[:, :ML_W] * ML_DH ** -0.5).astype(BF16)
    mk_ref[0] = y[:, ML_W:].astype(BF16)
    mv_ref[0] = proj(wm_ref, 2 * ML_W, 3 * ML_W).astype(BF16)
    mo_ref[0] = _sigmoid(proj(wm_ref, 3 * ML_W, 4 * ML_W)).astype(BF16)
    gate_ref[0] = jnp.dot(h2, wg_ref[...], preferred_element_type=F32)


def _front(x, g1, wgu, wd, gm, wqt, wk, wvt, wm, wg, cw, *, tm, tk, ck):
    B, S, D = x.shape
    d_ff = wd.shape[0]
    tok = lambda w: pl.BlockSpec((1, tm, w), lambda b, i: (b, i, 0))
    bf = lambda w: jax.ShapeDtypeStruct((B, S, w), BF16)
    consts = [g1, wgu, wd, gm, wqt, wk, wvt, wm, wg, cw]
    return pl.pallas_call(
        functools.partial(_front_kernel, ck=ck),
        grid=(B, S // tm),
        in_specs=[tok(D)] + [_const_spec(c.shape) for c in consts],
        out_specs=[tok(D),
                   pl.BlockSpec((1, FOX_W, tm), lambda b, i: (b, 0, i)),
                   tok(FOX_W),
                   pl.BlockSpec((1, tm // tk, FOX_W, tk), lambda b, i: (b, i, 0, 0))]
                  + [tok(ML_W)] * 4 + [tok(LANES)],
        out_shape=[jax.ShapeDtypeStruct((B, S, D), F32),
                   jax.ShapeDtypeStruct((B, FOX_W, S), BF16),
                   bf(FOX_W),
                   jax.ShapeDtypeStruct((B, S // tk, FOX_W, tk), BF16)]
                  + [bf(ML_W)] * 4 + [jax.ShapeDtypeStruct((B, S, LANES), F32)],
        scratch_shapes=[pltpu.VMEM((tm, d_ff), BF16), pltpu.VMEM((tm + HALO, 2 * ML_W), F32)],
        compiler_params=pltpu.CompilerParams(
            dimension_semantics=("parallel", "arbitrary"), vmem_limit_bytes=VMEM_LIMIT),
        name="front",
    )(x, *consts)


FOX_F0 = 0
ML_I0 = FOX_HEADS
ML_F0 = FOX_HEADS + ML_HEADS


def _fox_aug_base(h):
    return FOX_DH if h % 2 == 0 else 0


def _placement_matrices():
    pqt = np.zeros((FOX_HEADS * LANES, AUG * LANES), np.float32)
    pk = np.zeros((AUG * LANES, FOX_HEADS * LANES), np.float32)
    for h in range(FOX_HEADS):
        base = h * LANES + _fox_aug_base(h)
        for t in range(AUG):
            pqt[base + t, t * LANES + FOX_F0 + h] = 1.0
            pk[t * LANES + FOX_F0 + h, base + AUG + t] = -1.0
    n = ML_HEADS * LANES
    p_m_stats = np.zeros((AUG * LANES, n), np.float32)
    p_a_stats = np.zeros_like(p_m_stats)
    p_fm_stats = np.zeros_like(p_m_stats)
    p_row = np.zeros_like(p_m_stats)
    p_col = np.zeros_like(p_m_stats)
    c_row = np.zeros((1, n), np.float32)
    c_col = np.zeros_like(c_row)
    for h in range(ML_HEADS):
        src = ML_F0 + h
        for t in range(AUG):
            p_m_stats[t * LANES + src, h * LANES + 0] = 1.0
            p_a_stats[t * LANES + src, h * LANES + 1] = 1.0
            p_fm_stats[t * LANES + src, h * LANES + 2] = 1.0
            p_row[t * LANES + src, h * LANES + AUG + t] = -1.0
            c_row[0, h * LANES + t] = 1.0
            p_col[t * LANES + src, h * LANES + t] = 1.0
            c_col[0, h * LANES + AUG + t] = 1.0
    as_bf = lambda a: jnp.asarray(a, BF16)
    as_f = lambda a: jnp.asarray(a, F32)
    return (as_bf(pqt), as_bf(pk),
            as_bf(p_m_stats), as_bf(p_a_stats), as_bf(p_fm_stats), as_bf(p_row), as_bf(p_col),
            as_f(c_row), as_f(c_col))


def _gates_kernel(gate_ref, bias_ref, fqt_ref, fk_ref,
                  pqt_ref, pk_ref, pms_ref, pas_ref, pfs_ref, prow_ref, pcol_ref,
                  crow_ref, ccol_ref,
                  qat_ref, ka_ref, stat_ref, row_ref, col_ref, carry_ref):
    tg = gate_ref.shape[1]

    @pl.when(pl.program_id(1) == 0)
    def _():
        carry_ref[...] = jnp.zeros(carry_ref.shape, F32)

    z = gate_ref[0] + bias_ref[...]
    lane = lax.broadcasted_iota(jnp.int32, (tg, LANES), 1)
    row = lax.broadcasted_iota(jnp.int32, (tg, LANES), 0)
    is_forget = (lane < ML_I0) | ((lane >= ML_F0) & (lane < ML_F0 + ML_HEADS))
    log_f = jnp.minimum(z, 0.0) - jnp.log1p(jnp.exp(-jnp.abs(z)))
    log_f = jnp.where(is_forget, log_f, 0.0)

    r = lax.broadcasted_iota(jnp.int32, (tg, tg), 0)
    c = lax.broadcasted_iota(jnp.int32, (tg, tg), 1)
    tril = jnp.where(r >= c, 1.0, 0.0).astype(BF16)
    s3 = _split3(log_f)
    cum = (jnp.dot(tril, s3[:, 0:LANES], preferred_element_type=F32)
           + jnp.dot(tril, s3[:, LANES:2 * LANES], preferred_element_type=F32)
           + jnp.dot(tril, s3[:, 2 * LANES:], preferred_element_type=F32))
    F = cum + carry_ref[0:1, :]
    carry_ref[0:1, :] = F[tg - 1:tg, :]

    i_pre = pltpu.roll(z, ML_F0 - ML_I0, axis=1)
    a = i_pre - F
    m = a
    sh = 1
    while sh < tg:
        m = jnp.maximum(m, jnp.where(row >= sh, pltpu.roll(m, sh, axis=0), -jnp.inf))
        sh *= 2
    M = jnp.maximum(m, carry_ref[1:2, :])
    carry_ref[1:2, :] = M[tg - 1:tg, :]

    f2 = F * LOG2E
    ka = jnp.dot(_split3(f2), pk_ref[...], preferred_element_type=F32)
    f2t = f2.T
    hi = f2t.astype(BF16)
    rem = f2t - hi.astype(F32)
    mid = rem.astype(BF16)
    lo = (rem - mid.astype(F32)).astype(BF16)
    qat = jnp.dot(pqt_ref[...], jnp.concatenate([hi, mid, lo], axis=0),
                  preferred_element_type=F32)
    feat = lax.broadcasted_iota(jnp.int32, (LANES, tg), 0)
    for h in range(FOX_HEADS):
        pair = slice((h // 2) * LANES, (h // 2 + 1) * LANES)
        mine = slice(h * LANES, (h + 1) * LANES)
        base = _fox_aug_base(h)
        own_k = (lane < FOX_DH) if h % 2 == 0 else (lane >= FOX_DH)
        ones_k = (lane >= base) & (lane < base + AUG)
        ka_ref[0, h] = jnp.where(own_k, fk_ref[0, :, pair],
                                 jnp.where(ones_k, 1.0, ka[:, mine]).astype(BF16))
        own_q = (feat < FOX_DH) if h % 2 == 0 else (feat >= FOX_DH)
        ones_q = (feat >= base + AUG) & (feat < base + 2 * AUG)
        qat_ref[0, h] = jnp.where(own_q, fqt_ref[0, pair, :],
                                  jnp.where(ones_q, 1.0, qat[mine, :]).astype(BF16))

    m3 = _split3(M)
    a3 = _split3(a)
    fm3 = _split3(F + M)
    stats = (jnp.dot(m3, pms_ref[...], preferred_element_type=F32)
             + jnp.dot(a3, pas_ref[...], preferred_element_type=F32)
             + jnp.dot(fm3, pfs_ref[...], preferred_element_type=F32))
    rowop = jnp.dot(m3, prow_ref[...], preferred_element_type=F32) + crow_ref[...]
    colop = jnp.dot(a3, pcol_ref[...], preferred_element_type=F32) + ccol_ref[...]
    for h in range(ML_HEADS):
        sl = slice(h * LANES, (h + 1) * LANES)
        stat_ref[0, h] = stats[:, sl]
        row_ref[0, h] = rowop[:, sl].astype(BF16)
        col_ref[0, h] = colop[:, sl].astype(BF16)


def _gates(gate, bias, fqt, fk, mats, *, tg):
    B, S, _ = gate.shape
    tok = lambda w: pl.BlockSpec((1, tg, w), lambda b, i: (b, i, 0))
    head = lambda nh: pl.BlockSpec((1, nh, tg, LANES), lambda b, i: (b, 0, i, 0))
    return pl.pallas_call(
        _gates_kernel,
        grid=(B, S // tg),
        in_specs=[tok(LANES), _const_spec(bias.shape),
                  pl.BlockSpec((1, FOX_W, tg), lambda b, i: (b, 0, i)), tok(FOX_W)]
                 + [_const_spec(m.shape) for m in mats],
        out_specs=[pl.BlockSpec((1, FOX_HEADS, LANES, tg), lambda b, i: (b, 0, 0, i)),
                   head(FOX_HEADS), head(ML_HEADS), head(ML_HEADS), head(ML_HEADS)],
        out_shape=[jax.ShapeDtypeStruct((B, FOX_HEADS, LANES, S), BF16),
                   jax.ShapeDtypeStruct((B, FOX_HEADS, S, LANES), BF16),
                   jax.ShapeDtypeStruct((B, ML_HEADS, S, LANES), F32),
                   jax.ShapeDtypeStruct((B, ML_HEADS, S, LANES), BF16),
                   jax.ShapeDtypeStruct((B, ML_HEADS, S, LANES), BF16)],
        scratch_shapes=[pltpu.VMEM((HALO, LANES), F32)],
        compiler_params=pltpu.CompilerParams(
            dimension_semantics=("parallel", "arbitrary"), vmem_limit_bytes=VMEM_LIMIT),
        name="gates",
    )(gate, bias, fqt, fk, *mats)


def _fox_kernel(qt_ref, k_ref, vt_ref, gain_ref, o_ref, sa_ref, sb_ref, *, tq, tk):
    assert tq == 2 * tk
    i = pl.program_id(2)
    qt = [qt_ref[0, 0], qt_ref[0, 1]]

    def scores(blk, dst_ref):
        start = pl.multiple_of(blk * tk, tk)
        for hh in range(2):
            dst_ref[hh] = jnp.dot(k_ref[0, hh, pl.ds(start, tk), :], qt[hh],
                                  preferred_element_type=F32)

    def update(blk, src_ref, carry, first_key):
        out = []
        for hh in range(2):
            m, l, acc = carry[hh]
            s = src_ref[hh]
            if first_key is not None:
                key = first_key + lax.broadcasted_iota(jnp.int32, (tk, tq), 0)
                qry = lax.broadcasted_iota(jnp.int32, (tk, tq), 1)
                s = jnp.where(key <= qry, s, -jnp.inf)
            m_new = jnp.maximum(m, jnp.max(s, axis=0, keepdims=True))
            p = jnp.exp2(s - m_new)
            alpha = jnp.exp2(m - m_new)
            l = alpha * l + jnp.sum(p, axis=0, keepdims=True)
            vb = vt_ref[0, blk, hh * FOX_DH:(hh + 1) * FOX_DH, :]
            acc = alpha * acc + jnp.dot(vb, p.astype(BF16), preferred_element_type=F32)
            out.append((m_new, l, acc))
        return tuple(out)

    def pair(jj, carry):
        scores(2 * jj + 1, sb_ref)
        carry = update(2 * jj, sa_ref, carry, None)
        scores(2 * jj + 2, sa_ref)
        return update(2 * jj + 1, sb_ref, carry, None)

    init = (jnp.full((1, tq), -jnp.inf, F32), jnp.zeros((1, tq), F32), jnp.zeros((FOX_DH, tq), F32))
    scores(0, sa_ref)
    carry = lax.fori_loop(0, i, pair, (init, init))
    scores(2 * i + 1, sb_ref)
    carry = update(2 * i, sa_ref, carry, 0)
    carry = update(2 * i + 1, sb_ref, carry, tk)

    ys = []
    for hh in range(2):
        _, l, acc = carry[hh]
        o = acc / l
        ms = jnp.mean(o * o, axis=0, keepdims=True)
        ys.append(o * lax.rsqrt(ms + EPS) * gain_ref[hh * FOX_DH:(hh + 1) * FOX_DH, :])
    o_ref[0] = jnp.concatenate(ys, axis=0).T.astype(o_ref.dtype)


def _fox(qat, ka, fvt, gain_t, *, tq, tk):
    B, _, _, S = qat.shape
    return pl.pallas_call(
        functools.partial(_fox_kernel, tq=tq, tk=tk),
        grid=(B, FOX_HEADS // 2, S // tq),
        in_specs=[pl.BlockSpec((1, 2, LANES, tq), lambda b, p, i: (b, p, 0, i)),
                  pl.BlockSpec((1, 2, S, LANES), lambda b, p, i: (b, p, 0, 0)),
                  pl.BlockSpec((1, S // tk, LANES, tk), lambda b, p, i: (b, 0, p, 0)),
                  pl.BlockSpec((LANES, tq), lambda b, p, i: (p, 0))],
        out_specs=pl.BlockSpec((1, tq, LANES), lambda b, p, i: (b, i, p)),
        out_shape=jax.ShapeDtypeStruct((B, S, FOX_W), BF16),
        scratch_shapes=[pltpu.VMEM((2, tk, tq), F32), pltpu.VMEM((2, tk, tq), F32)],
        compiler_params=pltpu.CompilerParams(
            dimension_semantics=("parallel", "parallel", "arbitrary"), vmem_limit_bytes=VMEM_LIMIT),
        name="fox",
    )(qat, ka, fvt, gain_t)


def _mlstm_kernel(q_ref, k_ref, v_ref, o_ref, stat_ref, row_ref, col_ref, gain_ref, y_ref,
                  c_ref, mprev_ref):
    L = q_ref.shape[1]

    @pl.when(pl.program_id(2) == 0)
    def _():
        c_ref[...] = jnp.zeros(c_ref.shape, F32)
        mprev_ref[...] = jnp.zeros(mprev_ref.shape, F32)

    q = q_ref[0]
    k = k_ref[0]
    lane2 = lax.broadcasted_iota(jnp.int32, (L, LANES), 1)
    v_aug = jnp.concatenate([v_ref[0], jnp.where(lane2 == 0, 1.0, 0.0).astype(BF16)], axis=-1)
    stat = stat_ref[0, 0]
    m_col = stat[:, 0:1]
    a_col = stat[:, 1:2]
    fm_col = stat[:, 2:3]
    m_prev = mprev_ref[0:1, 0:1]
    m_end = m_col[L - 1:L, :]

    r = lax.broadcasted_iota(jnp.int32, (L, L), 0)
    c = lax.broadcasted_iota(jnp.int32, (L, L), 1)
    d_arg = lax.dot_general(row_ref[0, 0], col_ref[0, 0], _NT, preferred_element_type=F32)
    decay = jnp.where(c <= r, jnp.exp(d_arg), 0.0)
    qk = lax.dot_general(q, k, _NT, preferred_element_type=F32) * decay
    intra = jnp.dot(qk.astype(BF16), v_aug, preferred_element_type=F32)
    inter = jnp.dot(q, c_ref[...].astype(BF16), preferred_element_type=F32)
    tot = jnp.exp(m_prev - m_col) * inter + intra
    num = tot[:, :ML_DH]
    nq = tot[:, ML_DH:ML_DH + 1]
    hh = num / jnp.maximum(jnp.abs(nq), jnp.exp(-fm_col))
    y = hh * lax.rsqrt(jnp.mean(hh * hh, axis=-1, keepdims=True) + EPS) * gain_ref[...]
    y_ref[0] = (o_ref[0].astype(F32) * y).astype(y_ref.dtype)

    kw = (k.astype(F32) * jnp.exp(a_col - m_end)).T.astype(BF16)
    c_ref[...] = (jnp.exp(m_prev - m_end) * c_ref[...]
                  + jnp.dot(kw, v_aug, preferred_element_type=F32))
    mprev_ref[...] = jnp.broadcast_to(m_end, mprev_ref.shape)


def _mlstm(mq, mk, mv, mo, stat, rowop, colop, gain, *, L):
    B, S, _ = mq.shape
    tok = pl.BlockSpec((1, L, ML_DH), lambda b, h, c: (b, c, h))
    head = pl.BlockSpec((1, 1, L, LANES), lambda b, h, c: (b, h, c, 0))
    return pl.pallas_call(
        _mlstm_kernel,
        grid=(B, ML_HEADS, S // L),
        in_specs=[tok, tok, tok, tok, head, head, head,
                  pl.BlockSpec((1, ML_DH), lambda b, h, c: (0, h))],
        out_specs=tok,
        out_shape=jax.ShapeDtypeStruct((B, S, ML_W), BF16),
        scratch_shapes=[pltpu.VMEM((ML_DH, 2 * ML_DH), F32), pltpu.VMEM((HALO, LANES), F32)],
        compiler_params=pltpu.CompilerParams(
            dimension_semantics=("parallel", "parallel", "arbitrary"), vmem_limit_bytes=VMEM_LIMIT),
        name="mlstm",
    )(mq, mk, mv, mo, stat, rowop, colop, gain)


def _back_kernel(x1_ref, yf_ref, ym_ref, wo_ref, g2_ref, wgu_ref, wd_ref, gf_ref, out_ref,
                 act_ref, *, ck):
    mix = (jnp.dot(yf_ref[0], wo_ref[0:FOX_W, :], preferred_element_type=F32)
           + jnp.dot(ym_ref[0], wo_ref[FOX_W:, :], preferred_element_type=F32))
    x2 = x1_ref[0] + mix
    h = _rms(x2, g2_ref[...]).astype(BF16)
    x3 = x2 + 0.5 * _swiglu(h, wgu_ref, wd_ref, act_ref, ck)
    out_ref[0] = _rms(x3, gf_ref[...])


def _back(x1, yf, ym, wo, g2, wgu, wd, gf, *, tm, ck):
    B, S, D = x1.shape
    d_ff = wd.shape[0]
    tok = lambda w: pl.BlockSpec((1, tm, w), lambda b, i: (b, i, 0))
    return pl.pallas_call(
        functools.partial(_back_kernel, ck=ck),
        grid=(B, S // tm),
        in_specs=[tok(D), tok(FOX_W), tok(ML_W), _const_spec(wo.shape), _const_spec(g2.shape),
                  _const_spec(wgu.shape), _const_spec(wd.shape), _const_spec(gf.shape)],
        out_specs=tok(D),
        out_shape=jax.ShapeDtypeStruct((B, S, D), F32),
        scratch_shapes=[pltpu.VMEM((tm, d_ff), BF16)],
        compiler_params=pltpu.CompilerParams(
            dimension_semantics=("parallel", "parallel"), vmem_limit_bytes=VMEM_LIMIT),
        name="back",
    )(x1, yf, ym, wo, g2, wgu, wd, gf)


def _layer(x, ffn1_norm, ffn1_w_gu, ffn1_w_down, mix_norm, w_in, fox_f_bias, mlstm_i_bias,
           mlstm_f_bias, conv_w, fox_out_norm, mlstm_out_norm, w_out, ffn2_norm, ffn2_w_gu,
           ffn2_w_down, final_norm, *, tm, tg, tq, tk, L, ck):
    row = lambda v: v.reshape(1, -1).astype(F32)
    bf = lambda w: w.astype(BF16)
    o_ff = 3 * FOX_W
    o_m = o_ff + FOX_HEADS
    o_mi = o_m + 4 * ML_W
    wqt = bf(w_in[:, :FOX_W].T)
    wk = bf(w_in[:, FOX_W:2 * FOX_W])
    wvt = bf(w_in[:, 2 * FOX_W:o_ff].T)
    wm = bf(w_in[:, o_m:o_mi])
    wg = jnp.concatenate([w_in[:, o_ff:o_m], w_in[:, o_mi:]], axis=1)
    wg = bf(jnp.pad(wg, ((0, 0), (0, LANES - N_GATES))))
    bias = jnp.pad(jnp.concatenate([fox_f_bias, mlstm_i_bias, mlstm_f_bias]), (0, LANES - N_GATES))

    x1, fqt, fk, fvt, mq, mk, mv, mo, gate = _front(
        x, row(ffn1_norm), bf(ffn1_w_gu), bf(ffn1_w_down), row(mix_norm), wqt, wk, wvt, wm, wg,
        conv_w.astype(F32), tm=tm, tk=tk, ck=ck)
    qat, ka, stat, rowop, colop = _gates(gate, row(bias), fqt, fk, _placement_matrices(), tg=tg)
    fox_gain_t = jnp.broadcast_to(fox_out_norm.astype(F32)[:, None], (FOX_W, tq))
    yf = _fox(qat, ka, fvt, fox_gain_t, tq=tq, tk=tk)
    ym = _mlstm(mq, mk, mv, mo, stat, rowop, colop, row(mlstm_out_norm), L=L)
    return _back(x1, yf, ym, bf(w_out), row(ffn2_norm), bf(ffn2_w_gu), bf(ffn2_w_down),
                 row(final_norm), tm=tm, ck=ck)


def kernel(x, ffn1_norm, ffn1_w_gu, ffn1_w_down, mix_norm, w_in, fox_f_bias, mlstm_i_bias,
           mlstm_f_bias, conv_w, fox_out_norm, mlstm_out_norm, w_out, ffn2_norm, ffn2_w_gu,
           ffn2_w_down, final_norm):
    assert ffn1_norm.shape[0] == 1, "the back kernel fuses the final norm: depth 1 only"
    return _layer(x, ffn1_norm[0], ffn1_w_gu[0], ffn1_w_down[0], mix_norm[0], w_in[0],
                  fox_f_bias[0], mlstm_i_bias[0], mlstm_f_bias[0], conv_w[0], fox_out_norm[0],
                  mlstm_out_norm[0], w_out[0], ffn2_norm[0], ffn2_w_gu[0], ffn2_w_down[0],
                  final_norm, **TILES)
```

```python
import functools

import jax
import jax.numpy as jnp
from jax import lax
from jax.experimental import pallas as pl
from jax.experimental.pallas import tpu as pltpu

F32 = jnp.float32
BF16 = jnp.bfloat16
EPS = 1e-6
LOG2E = 1.4426950408889634

FOX_HEADS = 8
FOX_DH = 64
FOX_W = FOX_HEADS * FOX_DH
ML_HEADS = 4
ML_DH = 128
ML_W = ML_HEADS * ML_DH
CONV_K = 4
LANES = 128
SUB = 8
SUB_BF16 = 16
AUG = 3

VMEM_LIMIT = 56 * 1024 * 1024

TILES = dict(tm=512, tq=512, tk=256, L=256, ck=256)

_NT = (((1,), (1,)), ((), ()))

G_FOX = 0
G_MLF = SUB
G_MLI = 2 * SUB
G_ROWS = 4 * SUB
TOK_A = AUG * FOX_HEADS


def _const_spec(shape):
    nd = len(shape)
    return pl.BlockSpec(shape, lambda *_: (0,) * nd, pipeline_mode=pl.Buffered(1))


def _rms(x, g):
    return x * lax.rsqrt(jnp.mean(x * x, axis=-1, keepdims=True) + EPS) * g


def _sigmoid(x):
    return 1.0 / (1.0 + jnp.exp(-x))


def _swiglu(h, wgu_ref, wd_ref, act_ref, ck):
    d_ff = wd_ref.shape[0]
    for c in range(d_ff // ck):
        g = jnp.dot(h, wgu_ref[:, c * ck:(c + 1) * ck], preferred_element_type=F32)
        u = jnp.dot(h, wgu_ref[:, d_ff + c * ck:d_ff + (c + 1) * ck], preferred_element_type=F32)
        act_ref[:, c * ck:(c + 1) * ck] = (g * _sigmoid(g) * u).astype(BF16)
    return jnp.dot(act_ref[...], wd_ref[...], preferred_element_type=F32)


def _terms(x):
    hi = x.astype(BF16).astype(F32)
    r = x - hi
    mid = r.astype(BF16).astype(F32)
    lo = (r - mid).astype(BF16).astype(F32)
    return hi, mid, lo


def _fox_aug_base(parity):
    return FOX_DH if parity == 0 else 0


def _front_kernel(x_ref, g1_ref, wgu_ref, wd_ref, gm_ref, wqt_ref, wk_ref, wvt_ref, wmqk_ref,
                  wmvt_ref, wmo_ref, wgt_ref, bias_ref, triu_ref, cw_ref,
                  x1_ref, fqt_ref, ka_ref, fvt_ref, fterm_ref, mqt_ref, mk_ref, mvt_ref, mo_ref,
                  stat_ref, tok_ref,
                  act_ref, cbuf_ref, carry_ref, *, ck):
    tm = x_ref.shape[1]
    tk = fvt_ref.shape[3]
    first = pl.program_id(1) == 0
    x = x_ref[0]
    h = _rms(x, g1_ref[...]).astype(BF16)
    x1 = x + 0.5 * _swiglu(h, wgu_ref, wd_ref, act_ref, ck)
    x1_ref[0] = x1
    h2 = _rms(x1, gm_ref[...]).astype(BF16)

    def proj(w_ref):
        return jnp.dot(h2, w_ref[...], preferred_element_type=F32)

    def proj_t(wt_ref):
        return lax.dot_general(wt_ref[...], h2, _NT, preferred_element_type=F32)

    fqt_ref[0] = (proj_t(wqt_ref) * (FOX_DH ** -0.5 * LOG2E)).astype(BF16)
    fvt = proj_t(wvt_ref).astype(BF16)
    for c in range(tm // tk):
        fvt_ref[0, c] = fvt[:, c * tk:(c + 1) * tk]
    fk = proj(wk_ref).astype(BF16)

    @pl.when(first)
    def _():
        cbuf_ref[0:SUB, :] = jnp.zeros((SUB, 2 * ML_W), F32)
        carry_ref[...] = jnp.zeros(carry_ref.shape, F32)

    cbuf_ref[SUB:SUB + tm, :] = proj(wmqk_ref)
    y = cbuf_ref[SUB:SUB + tm, :] * cw_ref[CONV_K - 1:CONV_K, :]
    for tap in range(CONV_K - 1):
        off = SUB - (CONV_K - 1) + tap
        y = y + cbuf_ref[off:off + tm, :] * cw_ref[tap:tap + 1, :]
    cbuf_ref[0:SUB, :] = cbuf_ref[tm:tm + SUB, :]
    y = y * _sigmoid(y)
    mqt_ref[0] = (y[:, :ML_W] * ML_DH ** -0.5).T.astype(BF16)
    mk_ref[0] = y[:, ML_W:].astype(BF16)
    mvt_ref[0] = proj_t(wmvt_ref).astype(BF16)
    mo_ref[0] = _sigmoid(proj(wmo_ref)).astype(BF16)

    z = proj_t(wgt_ref) + bias_ref[...]
    zf = z[0:G_MLI]
    log_f = jnp.minimum(zf, 0.0) - jnp.log1p(jnp.exp(-jnp.abs(zf)))
    cum = sum(jnp.dot(t.astype(BF16), triu_ref[...], preferred_element_type=F32)
              for t in _terms(log_f))
    f_nat = cum + carry_ref[0:G_MLI, 0:1]
    carry_ref[0:G_MLI, :] = jnp.broadcast_to(f_nat[:, tm - 1:tm], (G_MLI, LANES))
    F = f_nat * LOG2E

    f_fox = F[G_FOX:G_FOX + SUB]
    f_ml = F[G_MLF:G_MLF + SUB]
    a = z[G_MLI:G_MLI + SUB] * LOG2E - f_ml
    lane = lax.broadcasted_iota(jnp.int32, (SUB, tm), 1)
    m = a
    sh = 1
    while sh < tm:
        m = jnp.maximum(m, jnp.where(lane >= sh, pltpu.roll(m, sh, axis=1), -jnp.inf))
        sh *= 2
    M = jnp.maximum(m, carry_ref[G_MLI:G_MLI + SUB, 0:1])
    carry_ref[G_MLI:G_MLI + SUB, :] = jnp.broadcast_to(M[:, tm - 1:tm], (SUB, LANES))
    stat_ref[0, 0] = M
    stat_ref[0, 1] = a
    stat_ref[0, 2] = f_ml + M

    hi, mid, lo = _terms(f_fox)
    fterm_ref[0] = jnp.concatenate([hi, mid, lo], axis=0)
    stack = jnp.concatenate([-hi, -mid, -lo, a, jnp.zeros((LANES - 4 * SUB, tm), F32)], axis=0)
    tok = stack.T
    tok_ref[0] = tok

    lane_k = lax.broadcasted_iota(jnp.int32, (tm, LANES), 1)
    for hd in range(FOX_HEADS):
        base = _fox_aug_base(hd % 2)
        pair = slice((hd // 2) * LANES, (hd // 2 + 1) * LANES)
        own = (lane_k < FOX_DH) if hd % 2 == 0 else (lane_k >= FOX_DH)
        aug = jnp.where((lane_k >= base) & (lane_k < base + AUG), 1.0, 0.0)
        for t in range(AUG):
            aug = jnp.where(lane_k == base + AUG + t, tok[:, t * FOX_HEADS + hd:t * FOX_HEADS + hd + 1], aug)
        ka_ref[0, hd] = jnp.where(own, fk[:, pair], aug.astype(BF16))


def _front(x, g1, wgu, wd, gm, wqt, wk, wvt, wmqk, wmvt, wmo, wgt, bias_t, triu, cw, *, tm, tk, ck):
    B, S, D = x.shape
    d_ff = wd.shape[0]
    tok = lambda w: pl.BlockSpec((1, tm, w), lambda b, i: (b, i, 0))
    feat = lambda r: pl.BlockSpec((1, r, tm), lambda b, i: (b, 0, i))
    consts = [g1, wgu, wd, gm, wqt, wk, wvt, wmqk, wmvt, wmo, wgt, bias_t, triu, cw]
    return pl.pallas_call(
        functools.partial(_front_kernel, ck=ck),
        grid=(B, S // tm),
        in_specs=[tok(D)] + [_const_spec(c.shape) for c in consts],
        out_specs=[tok(D),
                   feat(FOX_W),
                   pl.BlockSpec((1, FOX_HEADS, tm, LANES), lambda b, i: (b, 0, i, 0)),
                   pl.BlockSpec((1, tm // tk, FOX_W, tk), lambda b, i: (b, i, 0, 0)),
                   feat(AUG * FOX_HEADS),
                   feat(ML_W), tok(ML_W), feat(ML_W), tok(ML_W),
                   pl.BlockSpec((1, 3, SUB, tm), lambda b, i: (b, 0, 0, i)),
                   tok(LANES)],
        out_shape=[jax.ShapeDtypeStruct((B, S, D), F32),
                   jax.ShapeDtypeStruct((B, FOX_W, S), BF16),
                   jax.ShapeDtypeStruct((B, FOX_HEADS, S, LANES), BF16),
                   jax.ShapeDtypeStruct((B, S // tk, FOX_W, tk), BF16),
                   jax.ShapeDtypeStruct((B, AUG * FOX_HEADS, S), F32),
                   jax.ShapeDtypeStruct((B, ML_W, S), BF16),
                   jax.ShapeDtypeStruct((B, S, ML_W), BF16),
                   jax.ShapeDtypeStruct((B, ML_W, S), BF16),
                   jax.ShapeDtypeStruct((B, S, ML_W), BF16),
                   jax.ShapeDtypeStruct((B, 3, SUB, S), F32),
                   jax.ShapeDtypeStruct((B, S, LANES), F32)],
        scratch_shapes=[pltpu.VMEM((tm, d_ff), BF16), pltpu.VMEM((tm + SUB, 2 * ML_W), F32),
                        pltpu.VMEM((G_MLI + SUB, LANES), F32)],
        compiler_params=pltpu.CompilerParams(
            dimension_semantics=("parallel", "arbitrary"), vmem_limit_bytes=VMEM_LIMIT),
        name="front",
    )(x, *consts)


def _fox_kernel(qt_ref, fterm_ref, k_ref, vt_ref, gain_ref, o_ref, sa_ref, sb_ref, *, tq, tk):
    nb = tq // tk
    assert nb * tk == tq and nb % 2 == 0
    UNROLL = 2 * nb
    i = pl.program_id(2)
    feat = lax.broadcasted_iota(jnp.int32, (LANES, tq), 0)
    qt = []
    for hh in range(2):
        hd = 2 * pl.program_id(1) + hh
        base = _fox_aug_base(hh)
        own = (feat < FOX_DH) if hh == 0 else (feat >= FOX_DH)
        aug = jnp.where((feat >= base + AUG) & (feat < base + 2 * AUG), 1.0, 0.0)
        for t in range(AUG):
            aug = jnp.where(feat == base + t, fterm_ref[0, pl.ds(t * FOX_HEADS + hd, 1), :], aug)
        qt.append(jnp.where(own, qt_ref[0], aug.astype(BF16)))
    ones_rows = jnp.where(lax.broadcasted_iota(jnp.int32, (SUB_BF16, tk), 0) == 0, 1.0, 0.0).astype(BF16)

    def scores(blk, dst_ref):
        start = pl.multiple_of(blk * tk, tk)
        for hh in range(2):
            dst_ref[hh] = jnp.dot(k_ref[0, hh, pl.ds(start, tk), :], qt[hh],
                                  preferred_element_type=F32)

    def update(blk, src_ref, carry, first_key):
        out = []
        for hh in range(2):
            m, l, acc = carry[hh]
            s = src_ref[hh]
            if first_key is not None:
                key = first_key + lax.broadcasted_iota(jnp.int32, (tk, tq), 0)
                qry = lax.broadcasted_iota(jnp.int32, (tk, tq), 1)
                s = jnp.where(key <= qry, s, -jnp.inf)
            m_new = jnp.maximum(m, jnp.max(s, axis=0, keepdims=True))
            p = jnp.exp2(s - m_new).astype(BF16)
            alpha = jnp.exp2(m - m_new)
            vb = jnp.concatenate([vt_ref[0, blk, hh * FOX_DH:(hh + 1) * FOX_DH, :], ones_rows], axis=0)
            pv = jnp.dot(vb, p, preferred_element_type=F32)
            l = alpha * l + pv[FOX_DH:FOX_DH + 1]
            acc = alpha * acc + pv[0:FOX_DH]
            out.append((m_new, l, acc))
        return tuple(out)

    slots = (sa_ref, sb_ref)

    def run(base, count, carry):
        for d in range(count):
            scores(base + d + 1, slots[(d + 1) % 2])
            carry = update(base + d, slots[d % 2], carry, None)
        return carry

    below = nb * i
    lead = below % UNROLL
    init = (jnp.full((1, tq), -jnp.inf, F32), jnp.zeros((1, tq), F32), jnp.zeros((FOX_DH, tq), F32))
    scores(0, sa_ref)
    carry = lax.cond(lead != 0, lambda c: run(0, nb, c), lambda c: c, (init, init))
    carry = lax.fori_loop(0, below // UNROLL, lambda jj, c: run(lead + UNROLL * jj, UNROLL, c), carry)
    for d in range(nb):
        if d + 1 < nb:
            scores(below + d + 1, slots[(d + 1) % 2])
        carry = update(below + d, slots[d % 2], carry, d * tk)

    ys = []
    for hh in range(2):
        _, l, acc = carry[hh]
        o = acc / l
        ms = jnp.mean(o * o, axis=0, keepdims=True)
        ys.append(o * lax.rsqrt(ms + EPS) * gain_ref[hh * FOX_DH:(hh + 1) * FOX_DH, :])
    o_ref[0] = jnp.concatenate(ys, axis=0).T.astype(o_ref.dtype)


def _fox(fqt, fterm, ka, fvt, gain_t, *, tq, tk):
    B, _, S = fqt.shape
    return pl.pallas_call(
        functools.partial(_fox_kernel, tq=tq, tk=tk),
        grid=(B, FOX_HEADS // 2, S // tq),
        in_specs=[pl.BlockSpec((1, LANES, tq), lambda b, p, i: (b, p, i)),
                  pl.BlockSpec((1, AUG * FOX_HEADS, tq), lambda b, p, i: (b, 0, i)),
                  pl.BlockSpec((1, 2, S, LANES), lambda b, p, i: (b, p, 0, 0)),
                  pl.BlockSpec((1, S // tk, LANES, tk), lambda b, p, i: (b, 0, p, 0)),
                  pl.BlockSpec((LANES, tq), lambda b, p, i: (p, 0))],
        out_specs=pl.BlockSpec((1, tq, LANES), lambda b, p, i: (b, i, p)),
        out_shape=jax.ShapeDtypeStruct((B, S, FOX_W), BF16),
        scratch_shapes=[pltpu.VMEM((2, tk, tq), F32), pltpu.VMEM((2, tk, tq), F32)],
        compiler_params=pltpu.CompilerParams(
            dimension_semantics=("parallel", "parallel", "arbitrary"), vmem_limit_bytes=VMEM_LIMIT),
        name="fox",
    )(fqt, fterm, ka, fvt, gain_t)


ML_ROWS = ML_DH + SUB_BF16


def _mlstm_kernel(qt_ref, k_ref, vt_ref, o_ref, stat_ref, tok_ref, gain_ref, y_ref,
                  c_ref, mprev_ref):
    L = k_ref.shape[1]

    @pl.when(pl.program_id(1) == 0)
    def _():
        c_ref[...] = jnp.zeros(c_ref.shape, F32)
        mprev_ref[...] = jnp.zeros(mprev_ref.shape, F32)

    causal = (lax.broadcasted_iota(jnp.int32, (L, L), 0) <= lax.broadcasted_iota(jnp.int32, (L, L), 1))
    ones_rows = jnp.where(lax.broadcasted_iota(jnp.int32, (SUB_BF16, L), 0) == 0, 1.0, 0.0).astype(BF16)
    tok = tok_ref[0]
    for hd in range(ML_HEADS):
        hs = slice(hd * ML_DH, (hd + 1) * ML_DH)
        qt = qt_ref[0, hs, :]
        k = k_ref[0, :, hs]
        v_aug = jnp.concatenate([vt_ref[0, hs, :], ones_rows], axis=0)
        m_row = stat_ref[0, 0, hd:hd + 1, :]
        a_row = stat_ref[0, 1, hd:hd + 1, :]
        fm_row = stat_ref[0, 2, hd:hd + 1, :]
        a_col = tok[:, TOK_A + hd:TOK_A + hd + 1]
        m_prev = mprev_ref[hd:hd + 1, 0:1]
        m_end = m_row[:, L - 1:L]

        w = jnp.dot(k, qt, preferred_element_type=F32)
        w = (w * jnp.where(causal, jnp.exp2(a_col - m_row), 0.0)).astype(BF16)
        tot = (jnp.exp2(m_prev - m_row) * jnp.dot(c_ref[hd].astype(BF16), qt, preferred_element_type=F32)
               + jnp.dot(v_aug, w, preferred_element_type=F32))
        nq = tot[ML_DH:ML_DH + 1]
        ht = tot[0:ML_DH] * (1.0 / jnp.maximum(jnp.abs(nq), jnp.exp2(-fm_row)))
        yt = ht * lax.rsqrt(jnp.mean(ht * ht, axis=0, keepdims=True) + EPS) * gain_ref[hs, :]
        y_ref[0, :, hs] = (o_ref[0, :, hs].astype(F32) * yt.T).astype(y_ref.dtype)

        vw = (v_aug.astype(F32) * jnp.exp2(a_row - m_end)).astype(BF16)
        c_ref[hd] = (jnp.exp2(m_prev - m_end) * c_ref[hd]
                     + jnp.dot(vw, k, preferred_element_type=F32))
        mprev_ref[hd:hd + 1, :] = jnp.broadcast_to(m_end, (1, LANES))


def _mlstm(mqt, mk, mvt, mo, stat, tok, gain_t, *, L):
    B, S, _ = mk.shape
    tokb = lambda w: pl.BlockSpec((1, L, w), lambda b, c: (b, c, 0))
    feat = pl.BlockSpec((1, ML_W, L), lambda b, c: (b, 0, c))
    return pl.pallas_call(
        _mlstm_kernel,
        grid=(B, S // L),
        in_specs=[feat, tokb(ML_W), feat, tokb(ML_W),
                  pl.BlockSpec((1, 3, SUB, L), lambda b, c: (b, 0, 0, c)),
                  tokb(LANES), _const_spec(gain_t.shape)],
        out_specs=tokb(ML_W),
        out_shape=jax.ShapeDtypeStruct((B, S, ML_W), BF16),
        scratch_shapes=[pltpu.VMEM((ML_HEADS, ML_ROWS, ML_DH), F32), pltpu.VMEM((SUB, LANES), F32)],
        compiler_params=pltpu.CompilerParams(
            dimension_semantics=("parallel", "arbitrary"), vmem_limit_bytes=VMEM_LIMIT),
        name="mlstm",
    )(mqt, mk, mvt, mo, stat, tok, gain_t)


def _back_kernel(x1_ref, yf_ref, ym_ref, wo_ref, g2_ref, wgu_ref, wd_ref, gf_ref, out_ref,
                 act_ref, *, ck):
    mix = (jnp.dot(yf_ref[0], wo_ref[0:FOX_W, :], preferred_element_type=F32)
           + jnp.dot(ym_ref[0], wo_ref[FOX_W:, :], preferred_element_type=F32))
    x2 = x1_ref[0] + mix
    h = _rms(x2, g2_ref[...]).astype(BF16)
    x3 = x2 + 0.5 * _swiglu(h, wgu_ref, wd_ref, act_ref, ck)
    out_ref[0] = _rms(x3, gf_ref[...])


def _back(x1, yf, ym, wo, g2, wgu, wd, gf, *, tm, ck):
    B, S, D = x1.shape
    d_ff = wd.shape[0]
    tok = lambda w: pl.BlockSpec((1, tm, w), lambda b, i: (b, i, 0))
    return pl.pallas_call(
        functools.partial(_back_kernel, ck=ck),
        grid=(B, S // tm),
        in_specs=[tok(D), tok(FOX_W), tok(ML_W), _const_spec(wo.shape), _const_spec(g2.shape),
                  _const_spec(wgu.shape), _const_spec(wd.shape), _const_spec(gf.shape)],
        out_specs=tok(D),
        out_shape=jax.ShapeDtypeStruct((B, S, D), F32),
        scratch_shapes=[pltpu.VMEM((tm, d_ff), BF16)],
        compiler_params=pltpu.CompilerParams(
            dimension_semantics=("parallel", "parallel"), vmem_limit_bytes=VMEM_LIMIT),
        name="back",
    )(x1, yf, ym, wo, g2, wgu, wd, gf)


def _layer(x, ffn1_norm, ffn1_w_gu, ffn1_w_down, mix_norm, w_in, fox_f_bias, mlstm_i_bias,
           mlstm_f_bias, conv_w, fox_out_norm, mlstm_out_norm, w_out, ffn2_norm, ffn2_w_gu,
           ffn2_w_down, final_norm, *, tm, tq, tk, L, ck):
    row = lambda v: v.reshape(1, -1).astype(F32)
    bf = lambda w: w.astype(BF16)
    o_ff = 3 * FOX_W
    o_m = o_ff + FOX_HEADS
    o_mi = o_m + 4 * ML_W
    o_mf = o_mi + ML_HEADS
    wqt = bf(w_in[:, :FOX_W].T)
    wk = bf(w_in[:, FOX_W:2 * FOX_W])
    wvt = bf(w_in[:, 2 * FOX_W:o_ff].T)
    wmqk = bf(w_in[:, o_m:o_m + 2 * ML_W])
    wmvt = bf(w_in[:, o_m + 2 * ML_W:o_m + 3 * ML_W].T)
    wmo = bf(w_in[:, o_m + 3 * ML_W:o_mi])

    def gate_rows(fox, mlf, mli):
        pad = lambda a, n: jnp.pad(a, ((0, n - a.shape[0]),) + ((0, 0),) * (a.ndim - 1))
        return jnp.concatenate([pad(fox, SUB), pad(mlf, SUB), pad(mli, 2 * SUB)], axis=0)

    wgt = bf(gate_rows(w_in[:, o_ff:o_m].T, w_in[:, o_mf:].T, w_in[:, o_mi:o_mf].T))
    bias = gate_rows(fox_f_bias, mlstm_f_bias, mlstm_i_bias).astype(F32)
    bias_t = jnp.broadcast_to(bias[:, None], (G_ROWS, tm))
    triu = jnp.triu(jnp.ones((tm, tm), BF16))

    x1, fqt, ka, fvt, fterm, mqt, mk, mvt, mo, stat, tok = _front(
        x, row(ffn1_norm), bf(ffn1_w_gu), bf(ffn1_w_down), row(mix_norm), wqt, wk, wvt, wmqk,
        wmvt, wmo, wgt, bias_t, triu, conv_w.astype(F32), tm=tm, tk=tk, ck=ck)
    fox_gain_t = jnp.broadcast_to(fox_out_norm.astype(F32)[:, None], (FOX_W, tq))
    yf = _fox(fqt, fterm, ka, fvt, fox_gain_t, tq=tq, tk=tk)
    ml_gain_t = jnp.broadcast_to(mlstm_out_norm.astype(F32)[:, None], (ML_W, L))
    ym = _mlstm(mqt, mk, mvt, mo, stat, tok, ml_gain_t, L=L)
    return _back(x1, yf, ym, bf(w_out), row(ffn2_norm), bf(ffn2_w_gu), bf(ffn2_w_down),
                 row(final_norm), tm=tm, ck=ck)


def kernel(x, ffn1_norm, ffn1_w_gu, ffn1_w_down, mix_norm, w_in, fox_f_bias, mlstm_i_bias,
           mlstm_f_bias, conv_w, fox_out_norm, mlstm_out_norm, w_out, ffn2_norm, ffn2_w_gu,
           ffn2_w_down, final_norm):
    assert ffn1_norm.shape[0] == 1, "the back kernel fuses the final norm: depth 1 only"
    return _layer(x, ffn1_norm[0], ffn1_w_gu[0], ffn1_w_down[0], mix_norm[0], w_in[0],
                  fox_f_bias[0], mlstm_i_bias[0], mlstm_f_bias[0], conv_w[0], fox_out_norm[0],
                  mlstm_out_norm[0], w_out[0], ffn2_norm[0], ffn2_w_gu[0], ffn2_w_down[0],
                  final_norm, **TILES)
```

```python
import functools

import jax
import jax.numpy as jnp
from jax import lax
from jax.experimental import pallas as pl
from jax.experimental.pallas import tpu as pltpu

F32 = jnp.float32
BF16 = jnp.bfloat16
EPS = 1e-6
LOG2E = 1.4426950408889634

FOX_HEADS = 8
FOX_DH = 64
FOX_W = FOX_HEADS * FOX_DH
ML_HEADS = 4
ML_DH = 128
ML_W = ML_HEADS * ML_DH
CONV_K = 4
LANES = 128
SUB = 8
SUB_BF16 = 16
AUG = 3

VMEM_LIMIT = 56 * 1024 * 1024

TILES = dict(tm=512, tq=512, tk=256, hps=4, L=256, ck=256)

_NT = (((1,), (1,)), ((), ()))

G_FOX = 0
G_MLF = SUB
G_MLI = 2 * SUB
G_ROWS = 4 * SUB
TOK_A = AUG * FOX_HEADS


def _const_spec(shape):
    nd = len(shape)
    return pl.BlockSpec(shape, lambda *_: (0,) * nd, pipeline_mode=pl.Buffered(1))


def _rms(x, g):
    return x * lax.rsqrt(jnp.mean(x * x, axis=-1, keepdims=True) + EPS) * g


def _sigmoid(x):
    return 1.0 / (1.0 + jnp.exp(-x))


def _swiglu(h, wgu_ref, wd_ref, act_ref, ck):
    d_ff = wd_ref.shape[0]
    for c in range(d_ff // ck):
        g = jnp.dot(h, wgu_ref[:, c * ck:(c + 1) * ck], preferred_element_type=F32)
        u = jnp.dot(h, wgu_ref[:, d_ff + c * ck:d_ff + (c + 1) * ck], preferred_element_type=F32)
        act_ref[:, c * ck:(c + 1) * ck] = (g * _sigmoid(g) * u).astype(BF16)
    return jnp.dot(act_ref[...], wd_ref[...], preferred_element_type=F32)


def _terms(x):
    hi = x.astype(BF16).astype(F32)
    r = x - hi
    mid = r.astype(BF16).astype(F32)
    lo = (r - mid).astype(BF16).astype(F32)
    return hi, mid, lo


def _fox_aug_base(parity):
    return FOX_DH if parity == 0 else 0


def _front_kernel(x_ref, g1_ref, wgu_ref, wd_ref, gm_ref, wqt_ref, wk_ref, wvt_ref, wmqk_ref,
                  wmvt_ref, wmo_ref, wgt_ref, bias_ref, triu_ref, cw_ref,
                  x1_ref, fqt_ref, ka_ref, fvt_ref, fterm_ref, mq_ref, mk_ref, mvt_ref, mo_ref,
                  stat_ref, tok_ref,
                  act_ref, cbuf_ref, carry_ref, *, ck):
    tm = x_ref.shape[1]
    tk = fvt_ref.shape[3]
    first = pl.program_id(1) == 0
    x = x_ref[0]
    h = _rms(x, g1_ref[...]).astype(BF16)
    x1 = x + 0.5 * _swiglu(h, wgu_ref, wd_ref, act_ref, ck)
    x1_ref[0] = x1
    h2 = _rms(x1, gm_ref[...]).astype(BF16)

    def proj(w_ref):
        return jnp.dot(h2, w_ref[...], preferred_element_type=F32)

    def proj_t(wt_ref):
        return lax.dot_general(wt_ref[...], h2, _NT, preferred_element_type=F32)

    @pl.when(first)
    def _():
        cbuf_ref[0:SUB, :] = jnp.zeros((SUB, 2 * ML_W), F32)
        carry_ref[...] = jnp.zeros(carry_ref.shape, F32)

    z = proj_t(wgt_ref) + bias_ref[...]
    cbuf_ref[SUB:SUB + tm, :] = proj(wmqk_ref)
    fk = proj(wk_ref).astype(BF16)

    zf = z[0:G_MLI]
    log_f = jnp.minimum(zf, 0.0) - jnp.log1p(jnp.exp(-jnp.abs(zf)))
    cum = sum(jnp.dot(t.astype(BF16), triu_ref[...], preferred_element_type=F32)
              for t in _terms(log_f))
    f_nat = cum + carry_ref[0:G_MLI, 0:1]
    carry_ref[0:G_MLI, :] = jnp.broadcast_to(f_nat[:, tm - 1:tm], (G_MLI, LANES))
    F = f_nat * LOG2E

    f_fox = F[G_FOX:G_FOX + SUB]
    f_ml = F[G_MLF:G_MLF + SUB]
    a = z[G_MLI:G_MLI + SUB] * LOG2E - f_ml
    lane = lax.broadcasted_iota(jnp.int32, (SUB, tm), 1)
    m = a
    sh = 1
    while sh < tm:
        m = jnp.maximum(m, jnp.where(lane >= sh, pltpu.roll(m, sh, axis=1), -jnp.inf))
        sh *= 2
    M = jnp.maximum(m, carry_ref[G_MLI:G_MLI + SUB, 0:1])
    carry_ref[G_MLI:G_MLI + SUB, :] = jnp.broadcast_to(M[:, tm - 1:tm], (SUB, LANES))
    stat_ref[0, 0] = M
    stat_ref[0, 1] = a
    stat_ref[0, 2] = f_ml + M

    hi, mid, lo = _terms(f_fox)
    fterm_ref[0] = jnp.concatenate([hi, mid, lo], axis=0)
    stack = jnp.concatenate([-hi, -mid, -lo, a, jnp.zeros((LANES - 4 * SUB, tm), F32)], axis=0)
    tok = stack.T
    tok_ref[0] = tok

    lane_k = lax.broadcasted_iota(jnp.int32, (tm, LANES), 1)
    for hd in range(FOX_HEADS):
        base = _fox_aug_base(hd % 2)
        pair = slice((hd // 2) * LANES, (hd // 2 + 1) * LANES)
        own = (lane_k < FOX_DH) if hd % 2 == 0 else (lane_k >= FOX_DH)
        aug = jnp.where((lane_k >= base) & (lane_k < base + AUG), 1.0, 0.0)
        for t in range(AUG):
            aug = jnp.where(lane_k == base + AUG + t, tok[:, t * FOX_HEADS + hd:t * FOX_HEADS + hd + 1], aug)
        ka_ref[0, hd] = jnp.where(own, fk[:, pair], aug.astype(BF16))

    y = cbuf_ref[SUB:SUB + tm, :] * cw_ref[CONV_K - 1:CONV_K, :]
    for tap in range(CONV_K - 1):
        off = SUB - (CONV_K - 1) + tap
        y = y + cbuf_ref[off:off + tm, :] * cw_ref[tap:tap + 1, :]
    cbuf_ref[0:SUB, :] = cbuf_ref[tm:tm + SUB, :]
    y = y * _sigmoid(y)
    mq_ref[0] = (y[:, :ML_W] * ML_DH ** -0.5).astype(BF16)
    mk_ref[0] = y[:, ML_W:].astype(BF16)

    fqt_ref[0] = (proj_t(wqt_ref) * (FOX_DH ** -0.5 * LOG2E)).astype(BF16)
    fvt = proj_t(wvt_ref).astype(BF16)
    for c in range(tm // tk):
        fvt_ref[0, c] = fvt[:, c * tk:(c + 1) * tk]
    mvt_ref[0] = proj_t(wmvt_ref).astype(BF16)
    mo_ref[0] = _sigmoid(proj(wmo_ref)).astype(BF16)


def _front(x, g1, wgu, wd, gm, wqt, wk, wvt, wmqk, wmvt, wmo, wgt, bias_t, triu, cw, *, tm, tk, ck):
    B, S, D = x.shape
    d_ff = wd.shape[0]
    tok = lambda w: pl.BlockSpec((1, tm, w), lambda b, i: (b, i, 0))
    feat = lambda r: pl.BlockSpec((1, r, tm), lambda b, i: (b, 0, i))
    consts = [g1, wgu, wd, gm, wqt, wk, wvt, wmqk, wmvt, wmo, wgt, bias_t, triu, cw]
    return pl.pallas_call(
        functools.partial(_front_kernel, ck=ck),
        grid=(B, S // tm),
        in_specs=[tok(D)] + [_const_spec(c.shape) for c in consts],
        out_specs=[tok(D),
                   feat(FOX_W),
                   pl.BlockSpec((1, FOX_HEADS, tm, LANES), lambda b, i: (b, 0, i, 0)),
                   pl.BlockSpec((1, tm // tk, FOX_W, tk), lambda b, i: (b, i, 0, 0)),
                   feat(AUG * FOX_HEADS),
                   tok(ML_W), tok(ML_W), feat(ML_W), tok(ML_W),
                   pl.BlockSpec((1, 3, SUB, tm), lambda b, i: (b, 0, 0, i)),
                   tok(LANES)],
        out_shape=[jax.ShapeDtypeStruct((B, S, D), F32),
                   jax.ShapeDtypeStruct((B, FOX_W, S), BF16),
                   jax.ShapeDtypeStruct((B, FOX_HEADS, S, LANES), BF16),
                   jax.ShapeDtypeStruct((B, S // tk, FOX_W, tk), BF16),
                   jax.ShapeDtypeStruct((B, AUG * FOX_HEADS, S), F32),
                   jax.ShapeDtypeStruct((B, S, ML_W), BF16),
                   jax.ShapeDtypeStruct((B, S, ML_W), BF16),
                   jax.ShapeDtypeStruct((B, ML_W, S), BF16),
                   jax.ShapeDtypeStruct((B, S, ML_W), BF16),
                   jax.ShapeDtypeStruct((B, 3, SUB, S), F32),
                   jax.ShapeDtypeStruct((B, S, LANES), F32)],
        scratch_shapes=[pltpu.VMEM((tm, d_ff), BF16), pltpu.VMEM((tm + SUB, 2 * ML_W), F32),
                        pltpu.VMEM((G_MLI + SUB, LANES), F32)],
        compiler_params=pltpu.CompilerParams(
            dimension_semantics=("parallel", "arbitrary"), vmem_limit_bytes=VMEM_LIMIT),
        name="front",
    )(x, *consts)


def _fox_kernel(qt_ref, fterm_ref, k_ref, vt_ref, gain_ref, o_ref, sa_ref, sb_ref, *, tq, tk, hps):
    nb = tq // tk
    assert nb * tk == tq and nb % 2 == 0 and hps % 2 == 0
    UNROLL = 2 * nb
    i = pl.program_id(2)
    heads = range(hps)
    feat = lax.broadcasted_iota(jnp.int32, (LANES, tq), 0)
    qt = []
    for hh in heads:
        hd = hps * pl.program_id(1) + hh
        base = _fox_aug_base(hh % 2)
        own = (feat < FOX_DH) if hh % 2 == 0 else (feat >= FOX_DH)
        aug = jnp.where((feat >= base + AUG) & (feat < base + 2 * AUG), 1.0, 0.0)
        for t in range(AUG):
            aug = jnp.where(feat == base + t, fterm_ref[0, pl.ds(t * FOX_HEADS + hd, 1), :], aug)
        pair = qt_ref[0, (hh // 2) * LANES:(hh // 2 + 1) * LANES, :]
        qt.append(jnp.where(own, pair, aug.astype(BF16)))
    ones_rows = jnp.where(lax.broadcasted_iota(jnp.int32, (SUB_BF16, tk), 0) == 0, 1.0, 0.0).astype(BF16)

    def scores(blk, dst_ref, hh):
        start = pl.multiple_of(blk * tk, tk)
        dst_ref[hh] = jnp.dot(k_ref[0, hh, pl.ds(start, tk), :], qt[hh],
                              preferred_element_type=F32)

    def update(blk, src_ref, state, hh, first_key):
        m, l, acc = state
        s = src_ref[hh]
        if first_key is not None:
            key = first_key + lax.broadcasted_iota(jnp.int32, (tk, tq), 0)
            qry = lax.broadcasted_iota(jnp.int32, (tk, tq), 1)
            s = jnp.where(key <= qry, s, -jnp.inf)
        m_new = jnp.maximum(m, jnp.max(s, axis=0, keepdims=True))
        p = jnp.exp2(s - m_new).astype(BF16)
        alpha = jnp.exp2(m - m_new)
        vb = jnp.concatenate([vt_ref[0, blk, hh * FOX_DH:(hh + 1) * FOX_DH, :], ones_rows], axis=0)
        pv = jnp.dot(vb, p, preferred_element_type=F32)
        return m_new, alpha * l + pv[FOX_DH:FOX_DH + 1], alpha * acc + pv[0:FOX_DH]

    slots = (sa_ref, sb_ref)

    def step(blk, parity, carry, first_key=None, ahead=True):
        out = []
        for hh in heads:
            if ahead:
                scores(blk + 1, slots[1 - parity], hh)
            out.append(update(blk, slots[parity], carry[hh], hh, first_key))
        return tuple(out)

    def run(base, count, carry):
        for d in range(count):
            carry = step(base + d, d % 2, carry)
        return carry

    below = nb * i
    lead = below % UNROLL
    init = (jnp.full((1, tq), -jnp.inf, F32), jnp.zeros((1, tq), F32), jnp.zeros((FOX_DH, tq), F32))
    for hh in heads:
        scores(0, sa_ref, hh)
    carry = lax.cond(lead != 0, lambda c: run(0, nb, c), lambda c: c, (init,) * hps)
    carry = lax.fori_loop(0, below // UNROLL, lambda jj, c: run(lead + UNROLL * jj, UNROLL, c), carry)
    for d in range(nb):
        carry = step(below + d, d % 2, carry, first_key=d * tk, ahead=d + 1 < nb)

    for hp in range(hps // 2):
        ys = []
        for hh in (2 * hp, 2 * hp + 1):
            _, l, acc = carry[hh]
            o = acc / l
            ms = jnp.mean(o * o, axis=0, keepdims=True)
            ys.append(o * lax.rsqrt(ms + EPS) * gain_ref[hh * FOX_DH:(hh + 1) * FOX_DH, :])
        o_ref[0, :, hp * LANES:(hp + 1) * LANES] = jnp.concatenate(ys, axis=0).T.astype(o_ref.dtype)


def _fox(fqt, fterm, ka, fvt, gain_t, *, tq, tk, hps):
    B, _, S = fqt.shape
    w = hps * FOX_DH
    return pl.pallas_call(
        functools.partial(_fox_kernel, tq=tq, tk=tk, hps=hps),
        grid=(B, FOX_HEADS // hps, S // tq),
        in_specs=[pl.BlockSpec((1, w, tq), lambda b, p, i: (b, p, i)),
                  pl.BlockSpec((1, AUG * FOX_HEADS, tq), lambda b, p, i: (b, 0, i)),
                  pl.BlockSpec((1, hps, S, LANES), lambda b, p, i: (b, p, 0, 0)),
                  pl.BlockSpec((1, S // tk, w, tk), lambda b, p, i: (b, 0, p, 0)),
                  pl.BlockSpec((w, tq), lambda b, p, i: (p, 0))],
        out_specs=pl.BlockSpec((1, tq, w), lambda b, p, i: (b, i, p)),
        out_shape=jax.ShapeDtypeStruct((B, S, FOX_W), BF16),
        scratch_shapes=[pltpu.VMEM((hps, tk, tq), F32), pltpu.VMEM((hps, tk, tq), F32)],
        compiler_params=pltpu.CompilerParams(
            dimension_semantics=("parallel", "parallel", "arbitrary"), vmem_limit_bytes=VMEM_LIMIT),
        name="fox",
    )(fqt, fterm, ka, fvt, gain_t)


ML_ROWS = ML_DH + SUB_BF16


def _mlstm_kernel(q_ref, k_ref, vt_ref, o_ref, stat_ref, tok_ref, gain_ref, y_ref,
                  c_ref, mprev_ref):
    L = k_ref.shape[1]

    @pl.when(pl.program_id(1) == 0)
    def _():
        c_ref[...] = jnp.zeros(c_ref.shape, F32)
        mprev_ref[...] = jnp.zeros(mprev_ref.shape, F32)

    causal = (lax.broadcasted_iota(jnp.int32, (L, L), 0) <= lax.broadcasted_iota(jnp.int32, (L, L), 1))
    ones_rows = jnp.where(lax.broadcasted_iota(jnp.int32, (SUB_BF16, L), 0) == 0, 1.0, 0.0).astype(BF16)
    tok = tok_ref[0]
    for hd in range(ML_HEADS):
        hs = slice(hd * ML_DH, (hd + 1) * ML_DH)
        q = q_ref[0, :, hs]
        k = k_ref[0, :, hs]
        v_aug = jnp.concatenate([vt_ref[0, hs, :], ones_rows], axis=0)
        m_row = stat_ref[0, 0, hd:hd + 1, :]
        a_row = stat_ref[0, 1, hd:hd + 1, :]
        fm_row = stat_ref[0, 2, hd:hd + 1, :]
        a_col = tok[:, TOK_A + hd:TOK_A + hd + 1]
        m_prev = mprev_ref[hd:hd + 1, 0:1]
        m_end = m_row[:, L - 1:L]

        w = lax.dot_general(k, q, _NT, preferred_element_type=F32)
        w = (w * jnp.where(causal, jnp.exp2(a_col - m_row), 0.0)).astype(BF16)
        inter = lax.dot_general(c_ref[hd].astype(BF16), q, _NT, preferred_element_type=F32)
        tot = jnp.exp2(m_prev - m_row) * inter + jnp.dot(v_aug, w, preferred_element_type=F32)
        nq = tot[ML_DH:ML_DH + 1]
        ht = tot[0:ML_DH] * (1.0 / jnp.maximum(jnp.abs(nq), jnp.exp2(-fm_row)))
        yt = ht * lax.rsqrt(jnp.mean(ht * ht, axis=0, keepdims=True) + EPS) * gain_ref[hs, :]
        y_ref[0, :, hs] = (o_ref[0, :, hs].astype(F32) * yt.T).astype(y_ref.dtype)

        vw = (v_aug.astype(F32) * jnp.exp2(a_row - m_end)).astype(BF16)
        c_ref[hd] = (jnp.exp2(m_prev - m_end) * c_ref[hd]
                     + jnp.dot(vw, k, preferred_element_type=F32))
        mprev_ref[hd:hd + 1, :] = jnp.broadcast_to(m_end, (1, LANES))


def _mlstm(mq, mk, mvt, mo, stat, tok, gain_t, *, L):
    B, S, _ = mk.shape
    tokb = lambda w: pl.BlockSpec((1, L, w), lambda b, c: (b, c, 0))
    feat = pl.BlockSpec((1, ML_W, L), lambda b, c: (b, 0, c))
    return pl.pallas_call(
        _mlstm_kernel,
        grid=(B, S // L),
        in_specs=[tokb(ML_W), tokb(ML_W), feat, tokb(ML_W),
                  pl.BlockSpec((1, 3, SUB, L), lambda b, c: (b, 0, 0, c)),
                  tokb(LANES), _const_spec(gain_t.shape)],
        out_specs=tokb(ML_W),
        out_shape=jax.ShapeDtypeStruct((B, S, ML_W), BF16),
        scratch_shapes=[pltpu.VMEM((ML_HEADS, ML_ROWS, ML_DH), F32), pltpu.VMEM((SUB, LANES), F32)],
        compiler_params=pltpu.CompilerParams(
            dimension_semantics=("parallel", "arbitrary"), vmem_limit_bytes=VMEM_LIMIT),
        name="mlstm",
    )(mq, mk, mvt, mo, stat, tok, gain_t)


def _back_kernel(x1_ref, yf_ref, ym_ref, wo_ref, g2_ref, wgu_ref, wd_ref, gf_ref, out_ref,
                 act_ref, *, ck):
    mix = (jnp.dot(yf_ref[0], wo_ref[0:FOX_W, :], preferred_element_type=F32)
           + jnp.dot(ym_ref[0], wo_ref[FOX_W:, :], preferred_element_type=F32))
    x2 = x1_ref[0] + mix
    h = _rms(x2, g2_ref[...]).astype(BF16)
    x3 = x2 + 0.5 * _swiglu(h, wgu_ref, wd_ref, act_ref, ck)
    out_ref[0] = _rms(x3, gf_ref[...])


def _back(x1, yf, ym, wo, g2, wgu, wd, gf, *, tm, ck):
    B, S, D = x1.shape
    d_ff = wd.shape[0]
    tok = lambda w: pl.BlockSpec((1, tm, w), lambda b, i: (b, i, 0))
    return pl.pallas_call(
        functools.partial(_back_kernel, ck=ck),
        grid=(B, S // tm),
        in_specs=[tok(D), tok(FOX_W), tok(ML_W), _const_spec(wo.shape), _const_spec(g2.shape),
                  _const_spec(wgu.shape), _const_spec(wd.shape), _const_spec(gf.shape)],
        out_specs=tok(D),
        out_shape=jax.ShapeDtypeStruct((B, S, D), F32),
        scratch_shapes=[pltpu.VMEM((tm, d_ff), BF16)],
        compiler_params=pltpu.CompilerParams(
            dimension_semantics=("parallel", "parallel"), vmem_limit_bytes=VMEM_LIMIT),
        name="back",
    )(x1, yf, ym, wo, g2, wgu, wd, gf)


def _layer(x, ffn1_norm, ffn1_w_gu, ffn1_w_down, mix_norm, w_in, fox_f_bias, mlstm_i_bias,
           mlstm_f_bias, conv_w, fox_out_norm, mlstm_out_norm, w_out, ffn2_norm, ffn2_w_gu,
           ffn2_w_down, final_norm, *, tm, tq, tk, hps, L, ck):
    row = lambda v: v.reshape(1, -1).astype(F32)
    bf = lambda w: w.astype(BF16)
    o_ff = 3 * FOX_W
    o_m = o_ff + FOX_HEADS
    o_mi = o_m + 4 * ML_W
    o_mf = o_mi + ML_HEADS
    wqt = bf(w_in[:, :FOX_W].T)
    wk = bf(w_in[:, FOX_W:2 * FOX_W])
    wvt = bf(w_in[:, 2 * FOX_W:o_ff].T)
    wmqk = bf(w_in[:, o_m:o_m + 2 * ML_W])
    wmvt = bf(w_in[:, o_m + 2 * ML_W:o_m + 3 * ML_W].T)
    wmo = bf(w_in[:, o_m + 3 * ML_W:o_mi])

    def gate_rows(fox, mlf, mli):
        pad = lambda a, n: jnp.pad(a, ((0, n - a.shape[0]),) + ((0, 0),) * (a.ndim - 1))
        return jnp.concatenate([pad(fox, SUB), pad(mlf, SUB), pad(mli, 2 * SUB)], axis=0)

    wgt = bf(gate_rows(w_in[:, o_ff:o_m].T, w_in[:, o_mf:].T, w_in[:, o_mi:o_mf].T))
    bias = gate_rows(fox_f_bias, mlstm_f_bias, mlstm_i_bias).astype(F32)
    bias_t = jnp.broadcast_to(bias[:, None], (G_ROWS, tm))
    triu = jnp.triu(jnp.ones((tm, tm), BF16))

    x1, fqt, ka, fvt, fterm, mq, mk, mvt, mo, stat, tok = _front(
        x, row(ffn1_norm), bf(ffn1_w_gu), bf(ffn1_w_down), row(mix_norm), wqt, wk, wvt, wmqk,
        wmvt, wmo, wgt, bias_t, triu, conv_w.astype(F32), tm=tm, tk=tk, ck=ck)
    fox_gain_t = jnp.broadcast_to(fox_out_norm.astype(F32)[:, None], (FOX_W, tq))
    yf = _fox(fqt, fterm, ka, fvt, fox_gain_t, tq=tq, tk=tk, hps=hps)
    ml_gain_t = jnp.broadcast_to(mlstm_out_norm.astype(F32)[:, None], (ML_W, L))
    ym = _mlstm(mq, mk, mvt, mo, stat, tok, ml_gain_t, L=L)
    return _back(x1, yf, ym, bf(w_out), row(ffn2_norm), bf(ffn2_w_gu), bf(ffn2_w_down),
                 row(final_norm), tm=tm, ck=ck)


def kernel(x, ffn1_norm, ffn1_w_gu, ffn1_w_down, mix_norm, w_in, fox_f_bias, mlstm_i_bias,
           mlstm_f_bias, conv_w, fox_out_norm, mlstm_out_norm, w_out, ffn2_norm, ffn2_w_gu,
           ffn2_w_down, final_norm):
    assert ffn1_norm.shape[0] == 1, "the back kernel fuses the final norm: depth 1 only"
    return _layer(x, ffn1_norm[0], ffn1_w_gu[0], ffn1_w_down[0], mix_norm[0], w_in[0],
                  fox_f_bias[0], mlstm_i_bias[0], mlstm_f_bias[0], conv_w[0], fox_out_norm[0],
                  mlstm_out_norm[0], w_out[0], ffn2_norm[0], ffn2_w_gu[0], ffn2_w_down[0],
                  final_norm, **TILES)
```

```python
import functools

import jax
import jax.numpy as jnp
from jax import lax
from jax.experimental import pallas as pl
from jax.experimental.pallas import tpu as pltpu

F32 = jnp.float32
BF16 = jnp.bfloat16
EPS = 1e-6
LOG2E = 1.4426950408889634

FOX_HEADS = 8
FOX_DH = 64
FOX_W = FOX_HEADS * FOX_DH
ML_HEADS = 4
ML_DH = 128
ML_W = ML_HEADS * ML_DH
CONV_K = 4
LANES = 128
SUB = 8
SUB_BF16 = 16
AUG = 3

VMEM_LIMIT = 56 * 1024 * 1024

TILES = dict(tm=512, tq=512, tk=256, hps=8, L=256, ck=256)

_NT = (((1,), (1,)), ((), ()))

G_FOX = 0
G_MLF = SUB
G_MLI = 2 * SUB
G_ROWS = 4 * SUB
TOK_A = AUG * FOX_HEADS


def _const_spec(shape):
    nd = len(shape)
    return pl.BlockSpec(shape, lambda *_: (0,) * nd, pipeline_mode=pl.Buffered(1))


def _rms(x, g):
    return x * lax.rsqrt(jnp.mean(x * x, axis=-1, keepdims=True) + EPS) * g


def _sigmoid(x):
    return 1.0 / (1.0 + jnp.exp(-x))


def _swiglu(h, wgu_ref, wd_ref, act_ref, ck):
    d_ff = wd_ref.shape[0]
    for c in range(d_ff // ck):
        g = jnp.dot(h, wgu_ref[:, c * ck:(c + 1) * ck], preferred_element_type=F32)
        u = jnp.dot(h, wgu_ref[:, d_ff + c * ck:d_ff + (c + 1) * ck], preferred_element_type=F32)
        act_ref[:, c * ck:(c + 1) * ck] = (g * _sigmoid(g) * u).astype(BF16)
    return jnp.dot(act_ref[...], wd_ref[...], preferred_element_type=F32)


def _terms(x):
    hi = x.astype(BF16).astype(F32)
    r = x - hi
    mid = r.astype(BF16).astype(F32)
    lo = (r - mid).astype(BF16).astype(F32)
    return hi, mid, lo


def _fox_aug_base(parity):
    return FOX_DH if parity == 0 else 0


def _front_kernel(x_ref, g1_ref, wgu_ref, wd_ref, gm_ref, wqt_ref, wk_ref, wvt_ref, wmqk_ref,
                  wmvt_ref, wmo_ref, wgt_ref, bias_ref, triu_ref, cw_ref,
                  x1_ref, fqt_ref, ka_ref, fvt_ref, fterm_ref, mq_ref, mk_ref, mvt_ref, mo_ref,
                  stat_ref, tok_ref,
                  act_ref, cbuf_ref, carry_ref, *, ck):
    tm = x_ref.shape[1]
    tk = fvt_ref.shape[3]
    first = pl.program_id(1) == 0
    x = x_ref[0]
    h = _rms(x, g1_ref[...]).astype(BF16)
    x1 = x + 0.5 * _swiglu(h, wgu_ref, wd_ref, act_ref, ck)
    x1_ref[0] = x1
    h2 = _rms(x1, gm_ref[...]).astype(BF16)

    def proj(w_ref):
        return jnp.dot(h2, w_ref[...], preferred_element_type=F32)

    def proj_t(wt_ref):
        return lax.dot_general(wt_ref[...], h2, _NT, preferred_element_type=F32)

    @pl.when(first)
    def _():
        cbuf_ref[0:SUB, :] = jnp.zeros((SUB, 2 * ML_W), F32)
        carry_ref[...] = jnp.zeros(carry_ref.shape, F32)

    z = proj_t(wgt_ref) + bias_ref[...]
    cbuf_ref[SUB:SUB + tm, :] = proj(wmqk_ref)
    fk = proj(wk_ref).astype(BF16)

    zf = z[0:G_MLI]
    log_f = jnp.minimum(zf, 0.0) - jnp.log1p(jnp.exp(-jnp.abs(zf)))
    cum = sum(jnp.dot(t.astype(BF16), triu_ref[...], preferred_element_type=F32)
              for t in _terms(log_f))
    f_nat = cum + carry_ref[0:G_MLI, 0:1]
    carry_ref[0:G_MLI, :] = jnp.broadcast_to(f_nat[:, tm - 1:tm], (G_MLI, LANES))
    F = f_nat * LOG2E

    f_fox = F[G_FOX:G_FOX + SUB]
    f_ml = F[G_MLF:G_MLF + SUB]
    a = z[G_MLI:G_MLI + SUB] * LOG2E - f_ml
    lane = lax.broadcasted_iota(jnp.int32, (SUB, tm), 1)
    m = a
    sh = 1
    while sh < tm:
        m = jnp.maximum(m, jnp.where(lane >= sh, pltpu.roll(m, sh, axis=1), -jnp.inf))
        sh *= 2
    M = jnp.maximum(m, carry_ref[G_MLI:G_MLI + SUB, 0:1])
    carry_ref[G_MLI:G_MLI + SUB, :] = jnp.broadcast_to(M[:, tm - 1:tm], (SUB, LANES))
    stat_ref[0, 0] = M
    stat_ref[0, 1] = a
    stat_ref[0, 2] = f_ml + M

    hi, mid, lo = _terms(f_fox)
    fterm_ref[0] = jnp.concatenate([hi, mid, lo], axis=0)
    stack = jnp.concatenate([-hi, -mid, -lo, a, jnp.zeros((LANES - 4 * SUB, tm), F32)], axis=0)
    tok = stack.T
    tok_ref[0] = tok

    lane_k = lax.broadcasted_iota(jnp.int32, (tm, LANES), 1)
    for hd in range(FOX_HEADS):
        base = _fox_aug_base(hd % 2)
        pair = slice((hd // 2) * LANES, (hd // 2 + 1) * LANES)
        own = (lane_k < FOX_DH) if hd % 2 == 0 else (lane_k >= FOX_DH)
        aug = jnp.where((lane_k >= base) & (lane_k < base + AUG), 1.0, 0.0)
        for t in range(AUG):
            aug = jnp.where(lane_k == base + AUG + t, tok[:, t * FOX_HEADS + hd:t * FOX_HEADS + hd + 1], aug)
        ka_ref[0, hd] = jnp.where(own, fk[:, pair], aug.astype(BF16))

    y = cbuf_ref[SUB:SUB + tm, :] * cw_ref[CONV_K - 1:CONV_K, :]
    for tap in range(CONV_K - 1):
        off = SUB - (CONV_K - 1) + tap
        y = y + cbuf_ref[off:off + tm, :] * cw_ref[tap:tap + 1, :]
    cbuf_ref[0:SUB, :] = cbuf_ref[tm:tm + SUB, :]
    y = y * _sigmoid(y)
    mq_ref[0] = (y[:, :ML_W] * ML_DH ** -0.5).astype(BF16)
    mk_ref[0] = y[:, ML_W:].astype(BF16)

    fqt_ref[0] = (proj_t(wqt_ref) * (FOX_DH ** -0.5 * LOG2E)).astype(BF16)
    fvt = proj_t(wvt_ref).astype(BF16)
    for c in range(tm // tk):
        fvt_ref[0, c] = fvt[:, c * tk:(c + 1) * tk]
    mvt_ref[0] = proj_t(wmvt_ref).astype(BF16)
    mo_ref[0] = _sigmoid(proj(wmo_ref)).astype(BF16)


def _front(x, g1, wgu, wd, gm, wqt, wk, wvt, wmqk, wmvt, wmo, wgt, bias_t, triu, cw, *, tm, tk, ck):
    B, S, D = x.shape
    d_ff = wd.shape[0]
    tok = lambda w: pl.BlockSpec((1, tm, w), lambda b, i: (b, i, 0))
    feat = lambda r: pl.BlockSpec((1, r, tm), lambda b, i: (b, 0, i))
    consts = [g1, wgu, wd, gm, wqt, wk, wvt, wmqk, wmvt, wmo, wgt, bias_t, triu, cw]
    return pl.pallas_call(
        functools.partial(_front_kernel, ck=ck),
        grid=(B, S // tm),
        in_specs=[tok(D)] + [_const_spec(c.shape) for c in consts],
        out_specs=[tok(D),
                   feat(FOX_W),
                   pl.BlockSpec((1, FOX_HEADS, tm, LANES), lambda b, i: (b, 0, i, 0)),
                   pl.BlockSpec((1, tm // tk, FOX_W, tk), lambda b, i: (b, i, 0, 0)),
                   feat(AUG * FOX_HEADS),
                   tok(ML_W), tok(ML_W), feat(ML_W), tok(ML_W),
                   pl.BlockSpec((1, 3, SUB, tm), lambda b, i: (b, 0, 0, i)),
                   tok(LANES)],
        out_shape=[jax.ShapeDtypeStruct((B, S, D), F32),
                   jax.ShapeDtypeStruct((B, FOX_W, S), BF16),
                   jax.ShapeDtypeStruct((B, FOX_HEADS, S, LANES), BF16),
                   jax.ShapeDtypeStruct((B, S // tk, FOX_W, tk), BF16),
                   jax.ShapeDtypeStruct((B, AUG * FOX_HEADS, S), F32),
                   jax.ShapeDtypeStruct((B, S, ML_W), BF16),
                   jax.ShapeDtypeStruct((B, S, ML_W), BF16),
                   jax.ShapeDtypeStruct((B, ML_W, S), BF16),
                   jax.ShapeDtypeStruct((B, S, ML_W), BF16),
                   jax.ShapeDtypeStruct((B, 3, SUB, S), F32),
                   jax.ShapeDtypeStruct((B, S, LANES), F32)],
        scratch_shapes=[pltpu.VMEM((tm, d_ff), BF16), pltpu.VMEM((tm + SUB, 2 * ML_W), F32),
                        pltpu.VMEM((G_MLI + SUB, LANES), F32)],
        compiler_params=pltpu.CompilerParams(
            dimension_semantics=("parallel", "arbitrary"), vmem_limit_bytes=VMEM_LIMIT),
        name="front",
    )(x, *consts)


def _fox_kernel(qt_ref, fterm_ref, k_ref, vt_ref, gain_ref, o_ref, sa_ref, sb_ref, *, tq, tk, hps):
    nb = tq // tk
    assert nb * tk == tq and nb % 2 == 0 and hps % 2 == 0
    UNROLL = 2 * nb
    i = pl.program_id(2)
    heads = range(hps)
    feat = lax.broadcasted_iota(jnp.int32, (LANES, tq), 0)
    qt = []
    for hh in heads:
        hd = hps * pl.program_id(1) + hh
        base = _fox_aug_base(hh % 2)
        own = (feat < FOX_DH) if hh % 2 == 0 else (feat >= FOX_DH)
        aug = jnp.where((feat >= base + AUG) & (feat < base + 2 * AUG), 1.0, 0.0)
        for t in range(AUG):
            aug = jnp.where(feat == base + t, fterm_ref[0, pl.ds(t * FOX_HEADS + hd, 1), :], aug)
        pair = qt_ref[0, (hh // 2) * LANES:(hh // 2 + 1) * LANES, :]
        qt.append(jnp.where(own, pair, aug.astype(BF16)))
    ones_rows = jnp.where(lax.broadcasted_iota(jnp.int32, (SUB_BF16, tk), 0) == 0, 1.0, 0.0).astype(BF16)

    def scores(blk, dst_ref, hh, lo=0):
        start = pl.multiple_of(blk * tk, tk)
        dst_ref[hh, :, lo:] = jnp.dot(k_ref[0, hh, pl.ds(start, tk), :], qt[hh][:, lo:],
                                      preferred_element_type=F32)

    def update(blk, src_ref, state, hh, lo=0, diagonal=False):
        m, l, acc = (a[:, lo:] for a in state)
        s = src_ref[hh, :, lo:]
        if diagonal:
            tri = (lax.broadcasted_iota(jnp.int32, (tk, tk), 0) <= lax.broadcasted_iota(jnp.int32, (tk, tk), 1))
            masked = jnp.where(tri, s[:, :tk], -jnp.inf)
            s = masked if lo + tk == tq else jnp.concatenate([masked, s[:, tk:]], axis=1)
        m_new = jnp.maximum(m, jnp.max(s, axis=0, keepdims=True))
        p = jnp.exp2(s - m_new).astype(BF16)
        alpha = jnp.exp2(m - m_new)
        vb = jnp.concatenate([vt_ref[0, blk, hh * FOX_DH:(hh + 1) * FOX_DH, :], ones_rows], axis=0)
        pv = jnp.dot(vb, p, preferred_element_type=F32)
        new = (m_new, alpha * l + pv[FOX_DH:FOX_DH + 1], alpha * acc + pv[0:FOX_DH])
        if lo == 0:
            return new
        return tuple(jnp.concatenate([a[:, :lo], b], axis=1) for a, b in zip(state, new))

    slots = (sa_ref, sb_ref)

    def step(blk, parity, carry, diag=None):
        lo = 0 if diag is None else diag * tk
        out = []
        for hh in heads:
            if diag is None:
                scores(blk + 1, slots[1 - parity], hh)
            elif diag + 1 < nb:
                scores(blk + 1, slots[1 - parity], hh, lo + tk)
            out.append(update(blk, slots[parity], carry[hh], hh, lo, diag is not None))
        return tuple(out)

    def run(base, count, carry):
        for d in range(count):
            carry = step(base + d, d % 2, carry)
        return carry

    below = nb * i
    lead = below % UNROLL
    init = (jnp.full((1, tq), -jnp.inf, F32), jnp.zeros((1, tq), F32), jnp.zeros((FOX_DH, tq), F32))
    for hh in heads:
        scores(0, sa_ref, hh)
    carry = lax.cond(lead != 0, lambda c: run(0, nb, c), lambda c: c, (init,) * hps)
    carry = lax.fori_loop(0, below // UNROLL, lambda jj, c: run(lead + UNROLL * jj, UNROLL, c), carry)
    for d in range(nb):
        carry = step(below + d, d % 2, carry, diag=d)

    for hp in range(hps // 2):
        ys = []
        for hh in (2 * hp, 2 * hp + 1):
            _, l, acc = carry[hh]
            o = acc / l
            ms = jnp.mean(o * o, axis=0, keepdims=True)
            ys.append(o * lax.rsqrt(ms + EPS) * gain_ref[hh * FOX_DH:(hh + 1) * FOX_DH, :])
        o_ref[0, :, hp * LANES:(hp + 1) * LANES] = jnp.concatenate(ys, axis=0).T.astype(o_ref.dtype)


def _fox(fqt, fterm, ka, fvt, gain_t, *, tq, tk, hps):
    B, _, S = fqt.shape
    w = hps * FOX_DH
    return pl.pallas_call(
        functools.partial(_fox_kernel, tq=tq, tk=tk, hps=hps),
        grid=(B, FOX_HEADS // hps, S // tq),
        in_specs=[pl.BlockSpec((1, w, tq), lambda b, p, i: (b, p, i)),
                  pl.BlockSpec((1, AUG * FOX_HEADS, tq), lambda b, p, i: (b, 0, i)),
                  pl.BlockSpec((1, hps, S, LANES), lambda b, p, i: (b, p, 0, 0)),
                  pl.BlockSpec((1, S // tk, w, tk), lambda b, p, i: (b, 0, p, 0)),
                  pl.BlockSpec((w, tq), lambda b, p, i: (p, 0))],
        out_specs=pl.BlockSpec((1, tq, w), lambda b, p, i: (b, i, p)),
        out_shape=jax.ShapeDtypeStruct((B, S, FOX_W), BF16),
        scratch_shapes=[pltpu.VMEM((hps, tk, tq), F32), pltpu.VMEM((hps, tk, tq), F32)],
        compiler_params=pltpu.CompilerParams(
            dimension_semantics=("parallel", "parallel", "arbitrary"), vmem_limit_bytes=VMEM_LIMIT),
        name="fox",
    )(fqt, fterm, ka, fvt, gain_t)


ML_ROWS = ML_DH + SUB_BF16


def _mlstm_kernel(q_ref, k_ref, vt_ref, o_ref, stat_ref, tok_ref, gain_ref, y_ref,
                  c_ref, mprev_ref):
    L = k_ref.shape[1]

    @pl.when(pl.program_id(1) == 0)
    def _():
        c_ref[...] = jnp.zeros(c_ref.shape, F32)
        mprev_ref[...] = jnp.zeros(mprev_ref.shape, F32)

    causal = (lax.broadcasted_iota(jnp.int32, (L, L), 0) <= lax.broadcasted_iota(jnp.int32, (L, L), 1))
    ones_rows = jnp.where(lax.broadcasted_iota(jnp.int32, (SUB_BF16, L), 0) == 0, 1.0, 0.0).astype(BF16)
    tok = tok_ref[0]
    heads = range(ML_HEADS)
    hs = [slice(hd * ML_DH, (hd + 1) * ML_DH) for hd in heads]
    q = [q_ref[0, :, hs[hd]] for hd in heads]
    k = [k_ref[0, :, hs[hd]] for hd in heads]
    v_aug = [jnp.concatenate([vt_ref[0, hs[hd], :], ones_rows], axis=0) for hd in heads]
    m_row = [stat_ref[0, 0, hd:hd + 1, :] for hd in heads]
    a_row = [stat_ref[0, 1, hd:hd + 1, :] for hd in heads]
    fm_row = [stat_ref[0, 2, hd:hd + 1, :] for hd in heads]
    m_prev = [mprev_ref[hd:hd + 1, 0:1] for hd in heads]
    m_end = [m_row[hd][:, L - 1:L] for hd in heads]

    qk = [lax.dot_general(k[hd], q[hd], _NT, preferred_element_type=F32) for hd in heads]
    inter = [lax.dot_general(c_ref[hd].astype(BF16), q[hd], _NT, preferred_element_type=F32)
             for hd in heads]
    tot = []
    for hd in heads:
        a_col = tok[:, TOK_A + hd:TOK_A + hd + 1]
        w = (qk[hd] * jnp.where(causal, jnp.exp2(a_col - m_row[hd]), 0.0)).astype(BF16)
        tot.append(jnp.exp2(m_prev[hd] - m_row[hd]) * inter[hd]
                   + jnp.dot(v_aug[hd], w, preferred_element_type=F32))
    for hd in heads:
        vw = (v_aug[hd].astype(F32) * jnp.exp2(a_row[hd] - m_end[hd])).astype(BF16)
        c_ref[hd] = (jnp.exp2(m_prev[hd] - m_end[hd]) * c_ref[hd]
                     + jnp.dot(vw, k[hd], preferred_element_type=F32))
        mprev_ref[hd:hd + 1, :] = jnp.broadcast_to(m_end[hd], (1, LANES))
    for hd in heads:
        nq = tot[hd][ML_DH:ML_DH + 1]
        ht = tot[hd][0:ML_DH] * (1.0 / jnp.maximum(jnp.abs(nq), jnp.exp2(-fm_row[hd])))
        yt = ht * lax.rsqrt(jnp.mean(ht * ht, axis=0, keepdims=True) + EPS) * gain_ref[hs[hd], :]
        y_ref[0, :, hs[hd]] = (o_ref[0, :, hs[hd]].astype(F32) * yt.T).astype(y_ref.dtype)


def _mlstm(mq, mk, mvt, mo, stat, tok, gain_t, *, L):
    B, S, _ = mk.shape
    tokb = lambda w: pl.BlockSpec((1, L, w), lambda b, c: (b, c, 0))
    feat = pl.BlockSpec((1, ML_W, L), lambda b, c: (b, 0, c))
    return pl.pallas_call(
        _mlstm_kernel,
        grid=(B, S // L),
        in_specs=[tokb(ML_W), tokb(ML_W), feat, tokb(ML_W),
                  pl.BlockSpec((1, 3, SUB, L), lambda b, c: (b, 0, 0, c)),
                  tokb(LANES), _const_spec(gain_t.shape)],
        out_specs=tokb(ML_W),
        out_shape=jax.ShapeDtypeStruct((B, S, ML_W), BF16),
        scratch_shapes=[pltpu.VMEM((ML_HEADS, ML_ROWS, ML_DH), F32), pltpu.VMEM((SUB, LANES), F32)],
        compiler_params=pltpu.CompilerParams(
            dimension_semantics=("parallel", "arbitrary"), vmem_limit_bytes=VMEM_LIMIT),
        name="mlstm",
    )(mq, mk, mvt, mo, stat, tok, gain_t)


def _back_kernel(x1_ref, yf_ref, ym_ref, wo_ref, g2_ref, wgu_ref, wd_ref, gf_ref, out_ref,
                 act_ref, *, ck):
    mix = (jnp.dot(yf_ref[0], wo_ref[0:FOX_W, :], preferred_element_type=F32)
           + jnp.dot(ym_ref[0], wo_ref[FOX_W:, :], preferred_element_type=F32))
    x2 = x1_ref[0] + mix
    h = _rms(x2, g2_ref[...]).astype(BF16)
    x3 = x2 + 0.5 * _swiglu(h, wgu_ref, wd_ref, act_ref, ck)
    out_ref[0] = _rms(x3, gf_ref[...])


def _back(x1, yf, ym, wo, g2, wgu, wd, gf, *, tm, ck):
    B, S, D = x1.shape
    d_ff = wd.shape[0]
    tok = lambda w: pl.BlockSpec((1, tm, w), lambda b, i: (b, i, 0))
    return pl.pallas_call(
        functools.partial(_back_kernel, ck=ck),
        grid=(B, S // tm),
        in_specs=[tok(D), tok(FOX_W), tok(ML_W), _const_spec(wo.shape), _const_spec(g2.shape),
                  _const_spec(wgu.shape), _const_spec(wd.shape), _const_spec(gf.shape)],
        out_specs=tok(D),
        out_shape=jax.ShapeDtypeStruct((B, S, D), F32),
        scratch_shapes=[pltpu.VMEM((tm, d_ff), BF16)],
        compiler_params=pltpu.CompilerParams(
            dimension_semantics=("parallel", "parallel"), vmem_limit_bytes=VMEM_LIMIT),
        name="back",
    )(x1, yf, ym, wo, g2, wgu, wd, gf)


def _layer(x, ffn1_norm, ffn1_w_gu, ffn1_w_down, mix_norm, w_in, fox_f_bias, mlstm_i_bias,
           mlstm_f_bias, conv_w, fox_out_norm, mlstm_out_norm, w_out, ffn2_norm, ffn2_w_gu,
           ffn2_w_down, final_norm, *, tm, tq, tk, hps, L, ck):
    row = lambda v: v.reshape(1, -1).astype(F32)
    bf = lambda w: w.astype(BF16)
    o_ff = 3 * FOX_W
    o_m = o_ff + FOX_HEADS
    o_mi = o_m + 4 * ML_W
    o_mf = o_mi + ML_HEADS
    wqt = bf(w_in[:, :FOX_W].T)
    wk = bf(w_in[:, FOX_W:2 * FOX_W])
    wvt = bf(w_in[:, 2 * FOX_W:o_ff].T)
    wmqk = bf(w_in[:, o_m:o_m + 2 * ML_W])
    wmvt = bf(w_in[:, o_m + 2 * ML_W:o_m + 3 * ML_W].T)
    wmo = bf(w_in[:, o_m + 3 * ML_W:o_mi])

    def gate_rows(fox, mlf, mli):
        pad = lambda a, n: jnp.pad(a, ((0, n - a.shape[0]),) + ((0, 0),) * (a.ndim - 1))
        return jnp.concatenate([pad(fox, SUB), pad(mlf, SUB), pad(mli, 2 * SUB)], axis=0)

    wgt = bf(gate_rows(w_in[:, o_ff:o_m].T, w_in[:, o_mf:].T, w_in[:, o_mi:o_mf].T))
    bias = gate_rows(fox_f_bias, mlstm_f_bias, mlstm_i_bias).astype(F32)
    bias_t = jnp.broadcast_to(bias[:, None], (G_ROWS, tm))
    triu = jnp.triu(jnp.ones((tm, tm), BF16))

    x1, fqt, ka, fvt, fterm, mq, mk, mvt, mo, stat, tok = _front(
        x, row(ffn1_norm), bf(ffn1_w_gu), bf(ffn1_w_down), row(mix_norm), wqt, wk, wvt, wmqk,
        wmvt, wmo, wgt, bias_t, triu, conv_w.astype(F32), tm=tm, tk=tk, ck=ck)
    fox_gain_t = jnp.broadcast_to(fox_out_norm.astype(F32)[:, None], (FOX_W, tq))
    yf = _fox(fqt, fterm, ka, fvt, fox_gain_t, tq=tq, tk=tk, hps=hps)
    ml_gain_t = jnp.broadcast_to(mlstm_out_norm.astype(F32)[:, None], (ML_W, L))
    ym = _mlstm(mq, mk, mvt, mo, stat, tok, ml_gain_t, L=L)
    return _back(x1, yf, ym, bf(w_out), row(ffn2_norm), bf(ffn2_w_gu), bf(ffn2_w_down),
                 row(final_norm), tm=tm, ck=ck)


def kernel(x, ffn1_norm, ffn1_w_gu, ffn1_w_down, mix_norm, w_in, fox_f_bias, mlstm_i_bias,
           mlstm_f_bias, conv_w, fox_out_norm, mlstm_out_norm, w_out, ffn2_norm, ffn2_w_gu,
           ffn2_w_down, final_norm):
    assert ffn1_norm.shape[0] == 1, "the back kernel fuses the final norm: depth 1 only"
    return _layer(x, ffn1_norm[0], ffn1_w_gu[0], ffn1_w_down[0], mix_norm[0], w_in[0],
                  fox_f_bias[0], mlstm_i_bias[0], mlstm_f_bias[0], conv_w[0], fox_out_norm[0],
                  mlstm_out_norm[0], w_out[0], ffn2_norm[0], ffn2_w_gu[0], ffn2_w_down[0],
                  final_norm, **TILES)
```

```python
import functools

import jax
import jax.numpy as jnp
from jax import lax
from jax.experimental import pallas as pl
from jax.experimental.pallas import tpu as pltpu

F32 = jnp.float32
BF16 = jnp.bfloat16
EPS = 1e-6
LOG2E = 1.4426950408889634

FOX_HEADS = 8
FOX_DH = 64
FOX_W = FOX_HEADS * FOX_DH
ML_HEADS = 4
ML_DH = 128
ML_W = ML_HEADS * ML_DH
CONV_K = 4
LANES = 128
SUB = 8
SUB_BF16 = 16
AUG = 3

VMEM_LIMIT = 56 * 1024 * 1024

TILES = dict(tm=512, tq=512, tk=256, hps=8, L=256, cps=2, ck=256)

_NT = (((1,), (1,)), ((), ()))

G_FOX = 0
G_MLF = SUB
G_MLI = 2 * SUB
G_ROWS = 4 * SUB
TOK_A = AUG * FOX_HEADS


def _const_spec(shape):
    nd = len(shape)
    return pl.BlockSpec(shape, lambda *_: (0,) * nd, pipeline_mode=pl.Buffered(1))


def _rms(x, g):
    return x * lax.rsqrt(jnp.mean(x * x, axis=-1, keepdims=True) + EPS) * g


def _sigmoid(x):
    return 1.0 / (1.0 + jnp.exp(-x))


def _swiglu(h, wgu_ref, wd_ref, act_ref, ck):
    d_ff = wd_ref.shape[0]
    for c in range(d_ff // ck):
        g = jnp.dot(h, wgu_ref[:, c * ck:(c + 1) * ck], preferred_element_type=F32)
        u = jnp.dot(h, wgu_ref[:, d_ff + c * ck:d_ff + (c + 1) * ck], preferred_element_type=F32)
        act_ref[:, c * ck:(c + 1) * ck] = (g * _sigmoid(g) * u).astype(BF16)
    return jnp.dot(act_ref[...], wd_ref[...], preferred_element_type=F32)


def _terms(x):
    hi = x.astype(BF16).astype(F32)
    r = x - hi
    mid = r.astype(BF16).astype(F32)
    lo = (r - mid).astype(BF16).astype(F32)
    return hi, mid, lo


def _fox_aug_base(parity):
    return FOX_DH if parity == 0 else 0


def _front_kernel(x_ref, g1_ref, wgu_ref, wd_ref, gm_ref, wqt_ref, wk_ref, wvt_ref, wmqk_ref,
                  wmvt_ref, wmo_ref, wgt_ref, bias_ref, triu_ref, cw_ref,
                  x1_ref, fqt_ref, ka_ref, fvt_ref, fterm_ref, mq_ref, mk_ref, mvt_ref, mo_ref,
                  stat_ref, tok_ref,
                  act_ref, cbuf_ref, carry_ref, *, ck, nt, n_tiles):
    tm = x_ref.shape[1]
    t = pl.program_id(0)

    @pl.when(t == 0)
    def _():
        cbuf_ref[...] = jnp.zeros(cbuf_ref.shape, F32)

    @pl.when(jnp.maximum(t - 1, 0) % nt == 0)
    def _():
        cbuf_ref[0:SUB, :] = jnp.zeros((SUB, 2 * ML_W), F32)

    @pl.when(t % nt == 0)
    def _():
        carry_ref[...] = jnp.zeros(carry_ref.shape, F32)

    def conv_prev():
        y = cbuf_ref[SUB:SUB + tm, :] * cw_ref[CONV_K - 1:CONV_K, :]
        for tap in range(CONV_K - 1):
            off = SUB - (CONV_K - 1) + tap
            y = y + cbuf_ref[off:off + tm, :] * cw_ref[tap:tap + 1, :]
        cbuf_ref[0:SUB, :] = cbuf_ref[tm:tm + SUB, :]
        y = y * _sigmoid(y)
        mq_ref[0] = (y[:, :ML_W] * ML_DH ** -0.5).astype(BF16)
        mk_ref[0] = y[:, ML_W:].astype(BF16)

    @pl.when(t == n_tiles)
    def _():
        conv_prev()

    @pl.when(t < n_tiles)
    def _():
        conv_prev()
        _front_tile(x_ref, g1_ref, wgu_ref, wd_ref, gm_ref, wqt_ref, wk_ref, wvt_ref, wmqk_ref,
                    wmvt_ref, wmo_ref, wgt_ref, bias_ref, triu_ref,
                    x1_ref, fqt_ref, ka_ref, fvt_ref, fterm_ref, mvt_ref, mo_ref, stat_ref, tok_ref,
                    act_ref, cbuf_ref, carry_ref, ck=ck)


def _front_tile(x_ref, g1_ref, wgu_ref, wd_ref, gm_ref, wqt_ref, wk_ref, wvt_ref, wmqk_ref,
                wmvt_ref, wmo_ref, wgt_ref, bias_ref, triu_ref,
                x1_ref, fqt_ref, ka_ref, fvt_ref, fterm_ref, mvt_ref, mo_ref, stat_ref, tok_ref,
                act_ref, cbuf_ref, carry_ref, *, ck):
    tm = x_ref.shape[1]
    tk = fvt_ref.shape[3]
    x = x_ref[0]
    h = _rms(x, g1_ref[...]).astype(BF16)
    x1 = x + 0.5 * _swiglu(h, wgu_ref, wd_ref, act_ref, ck)
    x1_ref[0] = x1
    h2 = _rms(x1, gm_ref[...]).astype(BF16)

    def proj(w_ref):
        return jnp.dot(h2, w_ref[...], preferred_element_type=F32)

    def proj_t(wt_ref):
        return lax.dot_general(wt_ref[...], h2, _NT, preferred_element_type=F32)

    z = proj_t(wgt_ref) + bias_ref[...]
    cbuf_ref[SUB:SUB + tm, :] = proj(wmqk_ref)
    fk = proj(wk_ref).astype(BF16)

    zf = z[0:G_MLI]
    log_f = jnp.minimum(zf, 0.0) - jnp.log1p(jnp.exp(-jnp.abs(zf)))
    cum = sum(jnp.dot(t.astype(BF16), triu_ref[...], preferred_element_type=F32)
              for t in _terms(log_f))
    f_nat = cum + carry_ref[0:G_MLI, 0:1]
    carry_ref[0:G_MLI, :] = jnp.broadcast_to(f_nat[:, tm - 1:tm], (G_MLI, LANES))
    F = f_nat * LOG2E

    f_fox = F[G_FOX:G_FOX + SUB]
    f_ml = F[G_MLF:G_MLF + SUB]
    a = z[G_MLI:G_MLI + SUB] * LOG2E - f_ml
    lane = lax.broadcasted_iota(jnp.int32, (SUB, tm), 1)
    m = a
    sh = 1
    while sh < tm:
        m = jnp.maximum(m, jnp.where(lane >= sh, pltpu.roll(m, sh, axis=1), -jnp.inf))
        sh *= 2
    M = jnp.maximum(m, carry_ref[G_MLI:G_MLI + SUB, 0:1])
    carry_ref[G_MLI:G_MLI + SUB, :] = jnp.broadcast_to(M[:, tm - 1:tm], (SUB, LANES))
    stat_ref[0, 0] = M
    stat_ref[0, 1] = a
    stat_ref[0, 2] = f_ml + M

    hi, mid, lo = _terms(f_fox)
    fterm_ref[0] = jnp.concatenate([hi, mid, lo], axis=0)
    stack = jnp.concatenate([-hi, -mid, -lo, a, jnp.zeros((LANES - 4 * SUB, tm), F32)], axis=0)
    tok = stack.T
    tok_ref[0] = tok

    lane_k = lax.broadcasted_iota(jnp.int32, (tm, LANES), 1)
    for hd in range(FOX_HEADS):
        base = _fox_aug_base(hd % 2)
        pair = slice((hd // 2) * LANES, (hd // 2 + 1) * LANES)
        own = (lane_k < FOX_DH) if hd % 2 == 0 else (lane_k >= FOX_DH)
        aug = jnp.where((lane_k >= base) & (lane_k < base + AUG), 1.0, 0.0)
        for t in range(AUG):
            aug = jnp.where(lane_k == base + AUG + t, tok[:, t * FOX_HEADS + hd:t * FOX_HEADS + hd + 1], aug)
        ka_ref[0, hd] = jnp.where(own, fk[:, pair], aug.astype(BF16))

    fqt_ref[0] = (proj_t(wqt_ref) * (FOX_DH ** -0.5 * LOG2E)).astype(BF16)
    fvt = proj_t(wvt_ref).astype(BF16)
    for c in range(tm // tk):
        fvt_ref[0, c] = fvt[:, c * tk:(c + 1) * tk]
    mvt_ref[0] = proj_t(wmvt_ref).astype(BF16)
    mo_ref[0] = _sigmoid(proj(wmo_ref)).astype(BF16)


def _front(x, g1, wgu, wd, gm, wqt, wk, wvt, wmqk, wmvt, wmo, wgt, bias_t, triu, cw, *, tm, tk, ck):
    B, S, D = x.shape
    d_ff = wd.shape[0]
    nt = S // tm
    last = B * nt - 1
    tile = lambda t, lag: divmod(jnp.clip(t - lag, 0, last), nt)
    tok = lambda w, lag=0: pl.BlockSpec((1, tm, w), lambda t: (*tile(t, lag), 0))
    feat = lambda r: pl.BlockSpec((1, r, tm), lambda t: (tile(t, 0)[0], 0, tile(t, 0)[1]))
    consts = [g1, wgu, wd, gm, wqt, wk, wvt, wmqk, wmvt, wmo, wgt, bias_t, triu, cw]
    return pl.pallas_call(
        functools.partial(_front_kernel, ck=ck, nt=nt, n_tiles=B * nt),
        grid=(B * nt + 1,),
        in_specs=[tok(D)] + [_const_spec(c.shape) for c in consts],
        out_specs=[tok(D),
                   feat(FOX_W),
                   pl.BlockSpec((1, FOX_HEADS, tm, LANES),
                                lambda t: (tile(t, 0)[0], 0, tile(t, 0)[1], 0)),
                   pl.BlockSpec((1, tm // tk, FOX_W, tk), lambda t: (*tile(t, 0), 0, 0)),
                   feat(AUG * FOX_HEADS),
                   tok(ML_W, 1), tok(ML_W, 1), feat(ML_W), tok(ML_W),
                   pl.BlockSpec((1, 3, SUB, tm),
                                lambda t: (tile(t, 0)[0], 0, 0, tile(t, 0)[1])),
                   tok(LANES)],
        out_shape=[jax.ShapeDtypeStruct((B, S, D), F32),
                   jax.ShapeDtypeStruct((B, FOX_W, S), BF16),
                   jax.ShapeDtypeStruct((B, FOX_HEADS, S, LANES), BF16),
                   jax.ShapeDtypeStruct((B, S // tk, FOX_W, tk), BF16),
                   jax.ShapeDtypeStruct((B, AUG * FOX_HEADS, S), F32),
                   jax.ShapeDtypeStruct((B, S, ML_W), BF16),
                   jax.ShapeDtypeStruct((B, S, ML_W), BF16),
                   jax.ShapeDtypeStruct((B, ML_W, S), BF16),
                   jax.ShapeDtypeStruct((B, S, ML_W), BF16),
                   jax.ShapeDtypeStruct((B, 3, SUB, S), F32),
                   jax.ShapeDtypeStruct((B, S, LANES), F32)],
        scratch_shapes=[pltpu.VMEM((tm, d_ff), BF16), pltpu.VMEM((tm + SUB, 2 * ML_W), F32),
                        pltpu.VMEM((G_MLI + SUB, LANES), F32)],
        compiler_params=pltpu.CompilerParams(
            dimension_semantics=("arbitrary",), vmem_limit_bytes=VMEM_LIMIT),
        name="front",
    )(x, *consts)


def _fox_kernel(qt_ref, fterm_ref, k_ref, vt_ref, gain_ref, o_ref, sa_ref, sb_ref, *, tq, tk, hps):
    nb = tq // tk
    assert nb * tk == tq and nb % 2 == 0 and hps % 2 == 0
    UNROLL = 2 * nb
    i = pl.program_id(2)
    heads = range(hps)
    feat = lax.broadcasted_iota(jnp.int32, (LANES, tq), 0)
    qt = []
    for hh in heads:
        hd = hps * pl.program_id(1) + hh
        base = _fox_aug_base(hh % 2)
        own = (feat < FOX_DH) if hh % 2 == 0 else (feat >= FOX_DH)
        aug = jnp.where((feat >= base + AUG) & (feat < base + 2 * AUG), 1.0, 0.0)
        for t in range(AUG):
            aug = jnp.where(feat == base + t, fterm_ref[0, pl.ds(t * FOX_HEADS + hd, 1), :], aug)
        pair = qt_ref[0, (hh // 2) * LANES:(hh // 2 + 1) * LANES, :]
        qt.append(jnp.where(own, pair, aug.astype(BF16)))
    ones_rows = jnp.where(lax.broadcasted_iota(jnp.int32, (SUB_BF16, tk), 0) == 0, 1.0, 0.0).astype(BF16)

    def scores(blk, dst_ref, hh, lo=0):
        start = pl.multiple_of(blk * tk, tk)
        dst_ref[hh, :, lo:] = jnp.dot(k_ref[0, hh, pl.ds(start, tk), :], qt[hh][:, lo:],
                                      preferred_element_type=F32)

    def update(blk, src_ref, state, hh, lo=0, diagonal=False):
        m, l, acc = (a[:, lo:] for a in state)
        s = src_ref[hh, :, lo:]
        if diagonal:
            tri = (lax.broadcasted_iota(jnp.int32, (tk, tk), 0) <= lax.broadcasted_iota(jnp.int32, (tk, tk), 1))
            masked = jnp.where(tri, s[:, :tk], -jnp.inf)
            s = masked if lo + tk == tq else jnp.concatenate([masked, s[:, tk:]], axis=1)
        m_new = jnp.maximum(m, jnp.max(s, axis=0, keepdims=True))
        p = jnp.exp2(s - m_new).astype(BF16)
        alpha = jnp.exp2(m - m_new)
        vb = jnp.concatenate([vt_ref[0, blk, hh * FOX_DH:(hh + 1) * FOX_DH, :], ones_rows], axis=0)
        pv = jnp.dot(vb, p, preferred_element_type=F32)
        new = (m_new, alpha * l + pv[FOX_DH:FOX_DH + 1], alpha * acc + pv[0:FOX_DH])
        if lo == 0:
            return new
        return tuple(jnp.concatenate([a[:, :lo], b], axis=1) for a, b in zip(state, new))

    slots = (sa_ref, sb_ref)

    def step(blk, parity, carry, diag=None):
        lo = 0 if diag is None else diag * tk
        out = []
        for hh in heads:
            if diag is None:
                scores(blk + 1, slots[1 - parity], hh)
            elif diag + 1 < nb:
                scores(blk + 1, slots[1 - parity], hh, lo + tk)
            out.append(update(blk, slots[parity], carry[hh], hh, lo, diag is not None))
        return tuple(out)

    def run(base, count, carry):
        for d in range(count):
            carry = step(base + d, d % 2, carry)
        return carry

    below = nb * i
    lead = below % UNROLL
    init = (jnp.full((1, tq), -jnp.inf, F32), jnp.zeros((1, tq), F32), jnp.zeros((FOX_DH, tq), F32))
    for hh in heads:
        scores(0, sa_ref, hh)
    carry = lax.cond(lead != 0, lambda c: run(0, nb, c), lambda c: c, (init,) * hps)
    carry = lax.fori_loop(0, below // UNROLL, lambda jj, c: run(lead + UNROLL * jj, UNROLL, c), carry)
    for d in range(nb):
        carry = step(below + d, d % 2, carry, diag=d)

    for hp in range(hps // 2):
        ys = []
        for hh in (2 * hp, 2 * hp + 1):
            _, l, acc = carry[hh]
            o = acc / l
            ms = jnp.mean(o * o, axis=0, keepdims=True)
            ys.append(o * lax.rsqrt(ms + EPS) * gain_ref[hh * FOX_DH:(hh + 1) * FOX_DH, :])
        o_ref[0, :, hp * LANES:(hp + 1) * LANES] = jnp.concatenate(ys, axis=0).T.astype(o_ref.dtype)


def _fox(fqt, fterm, ka, fvt, gain_t, *, tq, tk, hps):
    B, _, S = fqt.shape
    w = hps * FOX_DH
    return pl.pallas_call(
        functools.partial(_fox_kernel, tq=tq, tk=tk, hps=hps),
        grid=(B, FOX_HEADS // hps, S // tq),
        in_specs=[pl.BlockSpec((1, w, tq), lambda b, p, i: (b, p, i)),
                  pl.BlockSpec((1, AUG * FOX_HEADS, tq), lambda b, p, i: (b, 0, i)),
                  pl.BlockSpec((1, hps, S, LANES), lambda b, p, i: (b, p, 0, 0)),
                  pl.BlockSpec((1, S // tk, w, tk), lambda b, p, i: (b, 0, p, 0)),
                  pl.BlockSpec((w, tq), lambda b, p, i: (p, 0))],
        out_specs=pl.BlockSpec((1, tq, w), lambda b, p, i: (b, i, p)),
        out_shape=jax.ShapeDtypeStruct((B, S, FOX_W), BF16),
        scratch_shapes=[pltpu.VMEM((hps, tk, tq), F32), pltpu.VMEM((hps, tk, tq), F32)],
        compiler_params=pltpu.CompilerParams(
            dimension_semantics=("parallel", "parallel", "arbitrary"), vmem_limit_bytes=VMEM_LIMIT),
        name="fox",
    )(fqt, fterm, ka, fvt, gain_t)


ML_ROWS = ML_DH + SUB_BF16


def _mlstm_kernel(q_ref, k_ref, vt_ref, o_ref, stat_ref, tok_ref, gain_ref, y_ref,
                  c_ref, mprev_ref, *, L):
    @pl.when(pl.program_id(1) == 0)
    def _():
        c_ref[...] = jnp.zeros(c_ref.shape, F32)
        mprev_ref[...] = jnp.zeros(mprev_ref.shape, F32)

    causal = (lax.broadcasted_iota(jnp.int32, (L, L), 0) <= lax.broadcasted_iota(jnp.int32, (L, L), 1))
    ones_rows = jnp.where(lax.broadcasted_iota(jnp.int32, (SUB_BF16, L), 0) == 0, 1.0, 0.0).astype(BF16)
    heads = range(ML_HEADS)
    hs = [slice(hd * ML_DH, (hd + 1) * ML_DH) for hd in heads]
    for ci in range(q_ref.shape[1] // L):
        fr = slice(ci * L, (ci + 1) * L)
        tok = tok_ref[0, fr, :]
        q = [q_ref[0, fr, hs[hd]] for hd in heads]
        k = [k_ref[0, fr, hs[hd]] for hd in heads]
        v_aug = [jnp.concatenate([vt_ref[0, hs[hd], fr], ones_rows], axis=0) for hd in heads]
        m_row = [stat_ref[0, 0, hd:hd + 1, fr] for hd in heads]
        a_row = [stat_ref[0, 1, hd:hd + 1, fr] for hd in heads]
        fm_row = [stat_ref[0, 2, hd:hd + 1, fr] for hd in heads]
        m_prev = [mprev_ref[hd:hd + 1, 0:1] for hd in heads]
        m_end = [m_row[hd][:, L - 1:L] for hd in heads]

        qk = [lax.dot_general(k[hd], q[hd], _NT, preferred_element_type=F32) for hd in heads]
        inter = [lax.dot_general(c_ref[hd].astype(BF16), q[hd], _NT, preferred_element_type=F32)
                 for hd in heads]
        tot = []
        for hd in heads:
            a_col = tok[:, TOK_A + hd:TOK_A + hd + 1]
            w = (qk[hd] * jnp.where(causal, jnp.exp2(a_col - m_row[hd]), 0.0)).astype(BF16)
            tot.append(jnp.exp2(m_prev[hd] - m_row[hd]) * inter[hd]
                       + jnp.dot(v_aug[hd], w, preferred_element_type=F32))
        for hd in heads:
            vw = (v_aug[hd].astype(F32) * jnp.exp2(a_row[hd] - m_end[hd])).astype(BF16)
            c_ref[hd] = (jnp.exp2(m_prev[hd] - m_end[hd]) * c_ref[hd]
                         + jnp.dot(vw, k[hd], preferred_element_type=F32))
            mprev_ref[hd:hd + 1, :] = jnp.broadcast_to(m_end[hd], (1, LANES))
        for hd in heads:
            nq = tot[hd][ML_DH:ML_DH + 1]
            ht = tot[hd][0:ML_DH] * (1.0 / jnp.maximum(jnp.abs(nq), jnp.exp2(-fm_row[hd])))
            yt = ht * lax.rsqrt(jnp.mean(ht * ht, axis=0, keepdims=True) + EPS) * gain_ref[hs[hd], :]
            y_ref[0, fr, hs[hd]] = (o_ref[0, fr, hs[hd]].astype(F32) * yt.T).astype(y_ref.dtype)


def _mlstm(mq, mk, mvt, mo, stat, tok, gain_t, *, L, cps):
    B, S, _ = mk.shape
    T = cps * L
    tokb = lambda w: pl.BlockSpec((1, T, w), lambda b, c: (b, c, 0))
    feat = pl.BlockSpec((1, ML_W, T), lambda b, c: (b, 0, c))
    return pl.pallas_call(
        functools.partial(_mlstm_kernel, L=L),
        grid=(B, S // T),
        in_specs=[tokb(ML_W), tokb(ML_W), feat, tokb(ML_W),
                  pl.BlockSpec((1, 3, SUB, T), lambda b, c: (b, 0, 0, c)),
                  tokb(LANES), _const_spec(gain_t.shape)],
        out_specs=tokb(ML_W),
        out_shape=jax.ShapeDtypeStruct((B, S, ML_W), BF16),
        scratch_shapes=[pltpu.VMEM((ML_HEADS, ML_ROWS, ML_DH), F32), pltpu.VMEM((SUB, LANES), F32)],
        compiler_params=pltpu.CompilerParams(
            dimension_semantics=("parallel", "arbitrary"), vmem_limit_bytes=VMEM_LIMIT),
        name="mlstm",
    )(mq, mk, mvt, mo, stat, tok, gain_t)


def _back_kernel(x1_ref, yf_ref, ym_ref, wo_ref, g2_ref, wgu_ref, wd_ref, gf_ref, out_ref,
                 act_ref, *, ck):
    mix = (jnp.dot(yf_ref[0], wo_ref[0:FOX_W, :], preferred_element_type=F32)
           + jnp.dot(ym_ref[0], wo_ref[FOX_W:, :], preferred_element_type=F32))
    x2 = x1_ref[0] + mix
    h = _rms(x2, g2_ref[...]).astype(BF16)
    x3 = x2 + 0.5 * _swiglu(h, wgu_ref, wd_ref, act_ref, ck)
    out_ref[0] = _rms(x3, gf_ref[...])


def _back(x1, yf, ym, wo, g2, wgu, wd, gf, *, tm, ck):
    B, S, D = x1.shape
    d_ff = wd.shape[0]
    tok = lambda w: pl.BlockSpec((1, tm, w), lambda b, i: (b, i, 0))
    return pl.pallas_call(
        functools.partial(_back_kernel, ck=ck),
        grid=(B, S // tm),
        in_specs=[tok(D), tok(FOX_W), tok(ML_W), _const_spec(wo.shape), _const_spec(g2.shape),
                  _const_spec(wgu.shape), _const_spec(wd.shape), _const_spec(gf.shape)],
        out_specs=tok(D),
        out_shape=jax.ShapeDtypeStruct((B, S, D), F32),
        scratch_shapes=[pltpu.VMEM((tm, d_ff), BF16)],
        compiler_params=pltpu.CompilerParams(
            dimension_semantics=("parallel", "parallel"), vmem_limit_bytes=VMEM_LIMIT),
        name="back",
    )(x1, yf, ym, wo, g2, wgu, wd, gf)


def _layer(x, ffn1_norm, ffn1_w_gu, ffn1_w_down, mix_norm, w_in, fox_f_bias, mlstm_i_bias,
           mlstm_f_bias, conv_w, fox_out_norm, mlstm_out_norm, w_out, ffn2_norm, ffn2_w_gu,
           ffn2_w_down, final_norm, *, tm, tq, tk, hps, L, cps, ck):
    row = lambda v: v.reshape(1, -1).astype(F32)
    bf = lambda w: w.astype(BF16)
    o_ff = 3 * FOX_W
    o_m = o_ff + FOX_HEADS
    o_mi = o_m + 4 * ML_W
    o_mf = o_mi + ML_HEADS
    wqt = bf(w_in[:, :FOX_W].T)
    wk = bf(w_in[:, FOX_W:2 * FOX_W])
    wvt = bf(w_in[:, 2 * FOX_W:o_ff].T)
    wmqk = bf(w_in[:, o_m:o_m + 2 * ML_W])
    wmvt = bf(w_in[:, o_m + 2 * ML_W:o_m + 3 * ML_W].T)
    wmo = bf(w_in[:, o_m + 3 * ML_W:o_mi])

    def gate_rows(fox, mlf, mli):
        pad = lambda a, n: jnp.pad(a, ((0, n - a.shape[0]),) + ((0, 0),) * (a.ndim - 1))
        return jnp.concatenate([pad(fox, SUB), pad(mlf, SUB), pad(mli, 2 * SUB)], axis=0)

    wgt = bf(gate_rows(w_in[:, o_ff:o_m].T, w_in[:, o_mf:].T, w_in[:, o_mi:o_mf].T))
    bias = gate_rows(fox_f_bias, mlstm_f_bias, mlstm_i_bias).astype(F32)
    bias_t = jnp.broadcast_to(bias[:, None], (G_ROWS, tm))
    triu = jnp.triu(jnp.ones((tm, tm), BF16))

    x1, fqt, ka, fvt, fterm, mq, mk, mvt, mo, stat, tok = _front(
        x, row(ffn1_norm), bf(ffn1_w_gu), bf(ffn1_w_down), row(mix_norm), wqt, wk, wvt, wmqk,
        wmvt, wmo, wgt, bias_t, triu, conv_w.astype(F32), tm=tm, tk=tk, ck=ck)
    fox_gain_t = jnp.broadcast_to(fox_out_norm.astype(F32)[:, None], (FOX_W, tq))
    yf = _fox(fqt, fterm, ka, fvt, fox_gain_t, tq=tq, tk=tk, hps=hps)
    ml_gain_t = jnp.broadcast_to(mlstm_out_norm.astype(F32)[:, None], (ML_W, L))
    ym = _mlstm(mq, mk, mvt, mo, stat, tok, ml_gain_t, L=L, cps=cps)
    return _back(x1, yf, ym, bf(w_out), row(ffn2_norm), bf(ffn2_w_gu), bf(ffn2_w_down),
                 row(final_norm), tm=tm, ck=ck)


def kernel(x, ffn1_norm, ffn1_w_gu, ffn1_w_down, mix_norm, w_in, fox_f_bias, mlstm_i_bias,
           mlstm_f_bias, conv_w, fox_out_norm, mlstm_out_norm, w_out, ffn2_norm, ffn2_w_gu,
           ffn2_w_down, final_norm):
    assert ffn1_norm.shape[0] == 1, "the back kernel fuses the final norm: depth 1 only"
    return _layer(x, ffn1_norm[0], ffn1_w_gu[0], ffn1_w_down[0], mix_norm[0], w_in[0],
                  fox_f_bias[0], mlstm_i_bias[0], mlstm_f_bias[0], conv_w[0], fox_out_norm[0],
                  mlstm_out_norm[0], w_out[0], ffn2_norm[0], ffn2_w_gu[0], ffn2_w_down[0],
                  final_norm, **TILES)
```
